```python
import jax, jax.numpy as jnp
from jax import lax
import numpy as np

D_MODEL = 1024
BATCH = 16
SEQ = 256
DEPTH = 4
DEC_BATCH = 2
DEC_SEQ = 2048
PAST_LEN = 512

GRID_W = 64
CHUNK = 128
D_MIX = D_MODEL
D_GMLP = D_MIX // 2
D_MLSTM = D_MIX - D_GMLP
H_GMLP = 4
H_MLSTM = 4
DH = D_MLSTM // H_MLSTM
CONV_K = 3
N_KEYS = 128
N_EXPERTS = N_KEYS * N_KEYS
PEER_HEADS = 8
PEER_TOPK = 16
D_KEY = 256
D_HALF = D_KEY // 2
N_GATES = 4 * H_MLSTM
D_IN = 2 * D_GMLP + 4 * D_MLSTM + N_GATES
SPLITS = [D_GMLP, 2 * D_GMLP, 2 * D_GMLP + 2 * D_MLSTM, 2 * D_GMLP + 3 * D_MLSTM, 2 * D_GMLP + 4 * D_MLSTM]
EPS = 1e-6

kernel_name = 'hybrid_gmlp_mlstm_peer_diffusion_step'


def rmsnorm(x, w=None):
    xf = x.astype(jnp.float32)
    y = xf * lax.rsqrt(jnp.mean(xf * xf, axis=-1, keepdims=True) + EPS)
    if w is not None:
        y = y * w.astype(jnp.float32)
    return y.astype(x.dtype)


def modulation(cond, w_mod, b_mod):
    m = jax.nn.silu(cond) @ w_mod + b_mod
    return jnp.split(m[:, None, :], 6, axis=-1)


def conv_ctx(x, w):
    S = x.shape[1]
    xp = jnp.pad(x, ((0, 0), (1, 1), (0, 0)))
    return sum(xp[:, j:j + S] * w[1, j] for j in range(CONV_K))


def conv_grid(x, w):
    B, S, C = x.shape
    rows = S // GRID_W
    xp = jnp.pad(x.reshape(B, rows, GRID_W, C), ((0, 0), (1, 1), (1, 1), (0, 0)))
    y = sum(xp[:, i:i + rows, j:j + GRID_W] * w[i, j] for i in range(CONV_K) for j in range(CONV_K))
    return y.reshape(B, S, C)


def chunk_gmlp(u, v, w_s, b_s):
    B, S, _ = u.shape
    nc = S // CHUNK
    vh = rmsnorm(v.reshape(B, S, H_GMLP, D_GMLP // H_GMLP)).reshape(B, nc, CHUNK, H_GMLP, D_GMLP // H_GMLP)
    mixed = jnp.einsum('hts,bcshd->bcthd', w_s, vh) + b_s.T[:, :, None]
    return u * mixed.reshape(B, S, D_GMLP)


def mlstm_scan(q, k, v, ig, lf, C0, n0, m0):
    B, H, S, _ = q.shape
    nc = S // CHUNK

    def to_chunks(a):
        return jnp.moveaxis(a.reshape(B, H, nc, CHUNK, *a.shape[3:]), 2, 0)

    xs = (to_chunks(q), to_chunks(k), to_chunks(v), to_chunks(ig), to_chunks(lf))
    causal = jnp.tril(jnp.ones((CHUNK, CHUNK), dtype=bool))

    def step(carry, inp):
        C, n, m = carry
        qc, kc, vc, ic, fc = inp
        b = jnp.cumsum(fc, axis=-1)
        dmat = jnp.where(causal, b[..., :, None] - b[..., None, :] + ic[..., None, :], -jnp.inf)
        prior = b + m[..., None]
        m_t = jnp.maximum(prior, jnp.max(dmat, axis=-1))
        w_intra = jnp.exp(dmat - m_t[..., None])
        w_prior = jnp.exp(prior - m_t)
        s = jnp.einsum('bhtd,bhsd->bhts', qc, kc) * w_intra
        num = w_prior[..., None] * jnp.einsum('bhtd,bhde->bhte', qc, C) + jnp.einsum('bhts,bhse->bhte', s, vc)
        den = w_prior * jnp.einsum('bhtd,bhd->bht', qc, n) + jnp.sum(s, axis=-1)
        h = num / jnp.maximum(jnp.abs(den), jnp.exp(-m_t))[..., None]
        bL = b[..., -1]
        log_end = bL[..., None] - b + ic
        m_new = jnp.maximum(bL + m, jnp.max(log_end, axis=-1))
        w_end = jnp.exp(log_end - m_new[..., None])
        decay = jnp.exp(bL + m - m_new)
        C_new = decay[..., None, None] * C + jnp.einsum('bhs,bhsd,bhse->bhde', w_end, kc, vc)
        n_new = decay[..., None] * n + jnp.einsum('bhs,bhsd->bhd', w_end, kc)
        return (C_new, n_new, m_new), h

    init = (C0.astype(jnp.float32), n0.astype(jnp.float32), m0.astype(jnp.float32))
    (C, n, m), hs = lax.scan(step, init, xs)
    return jnp.moveaxis(hs, 0, 2).reshape(B, H, S, DH), C, n, m


def token_mixers(h, conv_fn, C0, n0, m0, w_in, b_gate, conv_w, w_s, b_s, norm_w, w_out):
    B, S, _ = h.shape
    u, vs, qk, v, o, g = jnp.split(h @ w_in, SPLITS, axis=-1)
    y_a = chunk_gmlp(u, vs, w_s, b_s)
    q, k = jnp.split(jax.nn.silu(conv_fn(qk, conv_w)), 2, axis=-1)

    def heads(a):
        return a.reshape(B, S, H_MLSTM, DH).transpose(0, 2, 1, 3).astype(jnp.float32)

    q, k, v = heads(q), heads(k) * (DH ** -0.5), heads(v)
    gates = (g.reshape(B, S, 4, H_MLSTM).astype(jnp.float32) + b_gate.astype(jnp.float32)).transpose(2, 0, 3, 1)
    ig_f, lf_f = gates[0], jax.nn.log_sigmoid(gates[1])
    ig_b, lf_b = gates[2], jax.nn.log_sigmoid(gates[3])
    h_f, Cf, nf, mf = mlstm_scan(q, k, v, ig_f, lf_f, C0[:, 0], n0[:, 0], m0[:, 0])
    rev = lambda a: jnp.flip(a, axis=2)
    h_b, Cb, nb, mb = mlstm_scan(rev(q), rev(k), rev(v), rev(ig_b), rev(lf_b), C0[:, 1], n0[:, 1], m0[:, 1])
    hm = (h_f + rev(h_b)).transpose(0, 2, 1, 3)
    hm = rmsnorm(hm).reshape(B, S, D_MLSTM) * norm_w.astype(jnp.float32)
    y_b = hm.astype(h.dtype) * jax.nn.sigmoid(o)
    out = jnp.concatenate([y_a, y_b], axis=-1) @ w_out
    return out, (jnp.stack([Cf, Cb], axis=1), jnp.stack([nf, nb], axis=1), jnp.stack([mf, mb], axis=1))


def peer(h, w_pq, sub_keys, expert_u, expert_v):
    B, S, D = h.shape
    t = h.reshape(-1, D)
    T = t.shape[0]
    qh = (t @ w_pq).reshape(T, PEER_HEADS, 2, D_HALF).astype(jnp.float32)
    scores = jnp.einsum('thpd,hpkd->thpk', qh, sub_keys.astype(jnp.float32))
    s_top, i_top = lax.top_k(scores, PEER_TOPK)
    cand = s_top[:, :, 0, :, None] + s_top[:, :, 1, None, :]
    cand_idx = i_top[:, :, 0, :, None] * N_KEYS + i_top[:, :, 1, None, :]
    best, pos = lax.top_k(cand.reshape(T, PEER_HEADS, -1), PEER_TOPK)
    idx = jnp.take_along_axis(cand_idx.reshape(T, PEER_HEADS, -1), pos, axis=-1)
    gate = jax.nn.softmax(best, axis=-1).astype(h.dtype)

    def block(args):
        tb, ib, gb = args
        act = jax.nn.gelu(jnp.einsum('td,thkd->thk', tb, expert_u[ib]))
        return jnp.einsum('thk,thkd->td', gb * act, expert_v[ib])

    nb = T // CHUNK
    out = lax.map(block, (t.reshape(nb, CHUNK, D), idx.reshape(nb, CHUNK, PEER_HEADS, PEER_TOPK),
                          gate.reshape(nb, CHUNK, PEER_HEADS, PEER_TOPK)))
    return out.reshape(B, S, D)


def layer(x, mod, conv_fn, C0, n0, m0, w_in, b_gate, conv_w, w_s, b_s, norm_w, w_out,
          peer_wq, peer_keys, peer_u, peer_v):
    sh1, sc1, g1, sh2, sc2, g2 = mod
    hin = rmsnorm(x) * (1 + sc1) + sh1
    mix, state = token_mixers(hin, conv_fn, C0, n0, m0, w_in, b_gate, conv_w, w_s, b_s, norm_w, w_out)
    x = x + g1 * mix
    hin = rmsnorm(x) * (1 + sc2) + sh2
    x = x + g2 * peer(hin, peer_wq, peer_keys, peer_u, peer_v)
    return x, state


def setup_inputs(seed: int = 0) -> dict:
    key = jax.random.key(seed)
    ks = jax.random.split(key, 24)
    nrm = lambda k, shape, s: s * jax.random.normal(k, shape, jnp.float32)
    f_off = jnp.linspace(3.0, 6.0, H_MLSTM)
    z_off = jnp.zeros((H_MLSTM,), jnp.float32)
    gate_off = jnp.stack([z_off, f_off, z_off, f_off])
    return {
        'x_prompt': nrm(ks[0], (BATCH, SEQ, D_MODEL), 1.0),
        'x_sample': nrm(ks[1], (DEC_BATCH, DEC_SEQ, D_MODEL), 1.0),
        'state_C': nrm(ks[2], (DEC_BATCH, DEPTH, 2, H_MLSTM, DH, DH), DH ** -0.5),
        'state_n': nrm(ks[3], (DEC_BATCH, DEPTH, 2, H_MLSTM, DH), 1.0),
        'state_m': nrm(ks[4], (DEC_BATCH, DEPTH, 2, H_MLSTM), 0.5),
        'c': nrm(ks[5], (DEC_BATCH, D_MODEL), 1.0),
        'c_ctx': nrm(ks[6], (D_MODEL,), 1.0),
        'w_mod': nrm(ks[7], (DEPTH, D_MODEL, 6 * D_MODEL), 0.5 * D_MODEL ** -0.5),
        'b_mod': nrm(ks[8], (DEPTH, 6 * D_MODEL), 0.02),
        'w_in': nrm(ks[9], (DEPTH, D_MODEL, D_IN), D_MODEL ** -0.5),
        'b_gate': gate_off[None] + nrm(ks[10], (DEPTH, 4, H_MLSTM), 0.1),
        'conv_w': nrm(ks[11], (DEPTH, CONV_K, CONV_K, 2 * D_MLSTM), 0.3),
        'w_s': nrm(ks[12], (DEPTH, H_GMLP, CHUNK, CHUNK), CHUNK ** -0.5),
        'b_s': 1.0 + nrm(ks[13], (DEPTH, H_GMLP, CHUNK), 0.1),
        'mlstm_norm_w': 1.0 + nrm(ks[14], (DEPTH, D_MLSTM), 0.05),
        'w_out': nrm(ks[15], (DEPTH, D_MIX, D_MODEL), D_MIX ** -0.5),
        'peer_wq': nrm(ks[16], (DEPTH, D_MODEL, PEER_HEADS * D_KEY), D_MODEL ** -0.5),
        'peer_keys': nrm(ks[17], (DEPTH, PEER_HEADS, 2, N_KEYS, D_HALF), D_HALF ** -0.5),
        'peer_u': nrm(ks[18], (DEPTH, N_EXPERTS, D_MODEL), D_MODEL ** -0.5),
        'peer_v': nrm(ks[19], (DEPTH, N_EXPERTS, D_MODEL), 0.5),
        'final_norm_w': 1.0 + nrm(ks[20], (D_MODEL,), 0.05),
    }


def reference(x_prompt, x_sample, state_C, state_n, state_m, c, c_ctx, w_mod, b_mod, w_in, b_gate,
              conv_w, w_s, b_s, mlstm_norm_w, w_out, peer_wq, peer_keys, peer_u, peer_v, final_norm_w):
    B = x_prompt.shape[0]
    zC = jnp.zeros((B, 2, H_MLSTM, DH, DH), jnp.float32)
    zn = jnp.zeros((B, 2, H_MLSTM, DH), jnp.float32)
    zm = jnp.zeros((B, 2, H_MLSTM), jnp.float32)
    xp, xs = x_prompt, x_sample
    new_C, new_n, new_m = [], [], []
    for l in range(DEPTH):
        params = (w_in[l], b_gate[l], conv_w[l], w_s[l], b_s[l], mlstm_norm_w[l], w_out[l],
                  peer_wq[l], peer_keys[l], peer_u[l], peer_v[l])
        mod_ctx = modulation(c_ctx[None], w_mod[l], b_mod[l])
        mod_lat = modulation(c, w_mod[l], b_mod[l])
        xp, (Cl, nl, ml) = layer(xp, mod_ctx, conv_ctx, zC, zn, zm, *params)
        xs, _ = layer(xs, mod_lat, conv_grid, state_C[:, l], state_n[:, l], state_m[:, l], *params)
        new_C.append(Cl)
        new_n.append(nl)
        new_m.append(ml)
    y_prompt = rmsnorm(xp, final_norm_w)
    y_sample = rmsnorm(xs, final_norm_w)
    new_state_C = jnp.stack(new_C, axis=1).astype(x_prompt.dtype)
    new_state_n = jnp.stack(new_n, axis=1).astype(x_prompt.dtype)
    new_state_m = jnp.stack(new_m, axis=1).astype(x_prompt.dtype)
    return (y_prompt, y_sample, new_state_C, new_state_n, new_state_m)
```

```python
import functools

import jax
import jax.numpy as jnp
from jax import lax
from jax.experimental import pallas as pl
from jax.experimental.pallas import tpu as pltpu

F32 = jnp.float32
BF16 = jnp.bfloat16

D_MODEL = 1024
DEPTH = 4
N_CTX_SEQ, CTX_LEN = 16, 256
N_LAT_SEQ, LAT_LEN = 2, 2048
T_CTX = N_CTX_SEQ * CTX_LEN
T_LAT = N_LAT_SEQ * LAT_LEN
T_ALL = T_CTX + T_LAT
GRID_W = 64
GRID_H = LAT_LEN // GRID_W
CHUNK = 128
D_GMLP = 512
D_MLSTM = 512
N_HEADS = 4
DH = 128
N_KEYS = 128
N_EXPERTS = N_KEYS * N_KEYS
PEER_HEADS = 8
PEER_TOPK = 16
D_PROJ = 2 * D_GMLP + 4 * D_MLSTM
N_GATES = 16
EPS = 1e-6
NEG_INF = float("-inf")
POS_INF = float("inf")

VMEM_LIMIT_BYTES = 56 * 1024 * 1024

COL_U, COL_VS, COL_QK, COL_V, COL_O = 0, 512, 1024, 2048, 2560


def _dot(a, b):
    return jnp.dot(a, b, preferred_element_type=F32)


def _dot_nt(a, b):
    return lax.dot_general(a, b, (((1,), (1,)), ((), ())), preferred_element_type=F32)


def _dot_tn(a, b):
    return lax.dot_general(a, b, (((0,), (0,)), ((), ())), preferred_element_type=F32)


def _split_bf16(x):
    hi = x.astype(BF16)
    lo = (x - hi.astype(F32)).astype(BF16)
    return hi, lo


def _rms(x):
    return x * lax.rsqrt(jnp.mean(x * x, axis=-1, keepdims=True) + EPS)


def _sigmoid(x):
    return 1.0 / (1.0 + jnp.exp(-x))


def _gelu_tanh(x):
    return 0.5 * x * (1.0 + jnp.tanh(0.7978845608028654 * (x + 0.044715 * (x * x * x))))


def _segment(i, tile):
    start = i * tile
    return (start >= T_CTX).astype(jnp.int32) + (start >= T_CTX + LAT_LEN).astype(jnp.int32)


def _params(*sem):
    return pltpu.CompilerParams(dimension_semantics=sem, vmem_limit_bytes=VMEM_LIMIT_BYTES)


MOD_TN = 1536


def _mod_kernel(c_ref, w_ref, b_ref, o_ref):
    c = c_ref[...]
    a = (c * _sigmoid(c)).astype(BF16)
    o_ref[...] = _dot(a, w_ref[...].astype(BF16)) + b_ref[...]


def _modulation(cond, w_mod, b_mod):
    n = w_mod.shape[-1]
    return pl.pallas_call(
        _mod_kernel,
        out_shape=jax.ShapeDtypeStruct((DEPTH, 8, n), F32),
        grid=(DEPTH, n // MOD_TN),
        in_specs=[
            pl.BlockSpec((8, D_MODEL), lambda l, j: (0, 0)),
            pl.BlockSpec((None, D_MODEL, MOD_TN), lambda l, j: (l, 0, j)),
            pl.BlockSpec((None, 1, MOD_TN), lambda l, j: (l, 0, j)),
        ],
        out_specs=pl.BlockSpec((None, 8, MOD_TN), lambda l, j: (l, 0, j)),
        compiler_params=_params("arbitrary", "arbitrary"),
        name="modulation",
    )(cond, w_mod, b_mod.reshape(DEPTH, 1, n))


IN_TM = 512
IN_TN = 512
GATE_PAD = 128


def _inproj_kernel(x_ref, mod_ref, w_ref, wg_ref, wgt_ref, bg_ref, bgt_ref, y_ref, g_ref, gt_ref):
    x = x_ref[...]
    h = _rms(x) * (1.0 + mod_ref[1:2, :]) + mod_ref[0:1, :]
    hb = h.astype(BF16)
    for n0 in range(0, D_PROJ, IN_TN):
        y_ref[:, n0:n0 + IN_TN] = _dot(hb, w_ref[:, n0:n0 + IN_TN])

    def gate_act(g, idx):
        log_sig = jnp.minimum(g, 0.0) - jnp.log(1.0 + jnp.exp(-jnp.abs(g)))
        return jnp.where((idx & 4) != 0, log_sig, g)

    g = _dot(hb, wg_ref[...]) + bg_ref[...]
    g_ref[...] = gate_act(g, lax.broadcasted_iota(jnp.int32, g.shape, 1))
    gt = _dot_nt(wgt_ref[...], hb) + bgt_ref[...]
    gt_ref[...] = gate_act(gt, lax.broadcasted_iota(jnp.int32, gt.shape, 0))


def _inproj(x, mod_l, w_main, wg, wgt, bg, bgt):
    return pl.pallas_call(
        _inproj_kernel,
        out_shape=(
            jax.ShapeDtypeStruct((T_ALL, D_PROJ), F32),
            jax.ShapeDtypeStruct((T_ALL, GATE_PAD), F32),
            jax.ShapeDtypeStruct((N_GATES, T_ALL), F32),
        ),
        grid=(T_ALL // IN_TM,),
        in_specs=[
            pl.BlockSpec((IN_TM, D_MODEL), lambda i: (i, 0)),
            pl.BlockSpec((None, 6, D_MODEL), lambda i: (_segment(i, IN_TM), 0, 0)),
            pl.BlockSpec((D_MODEL, D_PROJ), lambda i: (0, 0)),
            pl.BlockSpec((D_MODEL, GATE_PAD), lambda i: (0, 0)),
            pl.BlockSpec((N_GATES, D_MODEL), lambda i: (0, 0)),
            pl.BlockSpec((1, GATE_PAD), lambda i: (0, 0)),
            pl.BlockSpec((N_GATES, 1), lambda i: (0, 0)),
        ],
        out_specs=(
            pl.BlockSpec((IN_TM, D_PROJ), lambda i: (i, 0)),
            pl.BlockSpec((IN_TM, GATE_PAD), lambda i: (i, 0)),
            pl.BlockSpec((N_GATES, IN_TM), lambda i: (0, i)),
        ),
        compiler_params=_params("arbitrary"),
        name="inproj",
    )(x, mod_l, w_main, wg, wgt, bg, bgt)


CONV_CB = 128
CONV_ROWS = 4096


def _conv_taps(x, w_ref, taps):
    rows = x.shape[0]
    acc = None
    for w_row, delta, valid in taps:
        shifted = x if delta == 0 else pltpu.roll(x, (-delta) % rows, 0)
        term = shifted * w_ref[w_row:w_row + 1, :]
        if valid is not None:
            term = jnp.where(valid, term, 0.0)
        acc = term if acc is None else acc + term
    return acc


def _conv_kernel(x_ref, w_ref, o_ref):
    grp = pl.program_id(0)
    cblk = pl.program_id(1)
    scale = jnp.where(cblk >= (D_MLSTM // CONV_CB), DH ** -0.5, 1.0).astype(F32)

    def finish(y):
        o_ref[...] = (y * _sigmoid(y)) * scale

    @pl.when(grp == 0)
    def _ctx():
        x = x_ref[...]
        pos = lax.broadcasted_iota(jnp.int32, x.shape, 0) & (CTX_LEN - 1)
        taps = [(3, -1, pos >= 1), (4, 0, None), (5, 1, pos <= CTX_LEN - 2)]
        finish(_conv_taps(x, w_ref, taps))

    @pl.when(grp == 1)
    def _lat():
        x = x_ref[...]
        pos = lax.broadcasted_iota(jnp.int32, x.shape, 0) & (LAT_LEN - 1)
        r = pos >> 6
        c = pos & (GRID_W - 1)
        row_ok = {-1: r >= 1, 0: None, 1: r <= GRID_H - 2}
        col_ok = {-1: c >= 1, 0: None, 1: c <= GRID_W - 2}
        taps = []
        for di in (-1, 0, 1):
            for dj in (-1, 0, 1):
                ok = row_ok[di]
                if col_ok[dj] is not None:
                    ok = col_ok[dj] if ok is None else (ok & col_ok[dj])
                taps.append(((di + 1) * 3 + (dj + 1), di * GRID_W + dj, ok))
        finish(_conv_taps(x, w_ref, taps))


def _conv(y, conv_w9):
    qk_blk0 = COL_QK // CONV_CB
    return pl.pallas_call(
        _conv_kernel,
        out_shape=jax.ShapeDtypeStruct((T_ALL, 2 * D_MLSTM), F32),
        grid=(T_ALL // CONV_ROWS, 2 * D_MLSTM // CONV_CB),
        in_specs=[
            pl.BlockSpec((CONV_ROWS, CONV_CB), lambda g, j: (g, qk_blk0 + j)),
            pl.BlockSpec((9, CONV_CB), lambda g, j: (0, j)),
        ],
        out_specs=pl.BlockSpec((CONV_ROWS, CONV_CB), lambda g, j: (g, j)),
        compiler_params=_params("arbitrary", "arbitrary"),
        name="conv_silu",
    )(y, conv_w9)


def _mlstm_direction(d, q_ref, k_ref, v_ref, g_ref, gt_ref, h_ref, cext_ref, m_ref):
    row = lax.broadcasted_iota(jnp.int32, (CHUNK, CHUNK), 0)
    col = lax.broadcasted_iota(jnp.int32, (CHUNK, CHUNK), 1)
    mask = (col <= row) if d == 0 else (col >= row)
    tri = jnp.where(mask, 1.0, 0.0).astype(BF16)
    g = g_ref[...]
    gt = gt_ref[...]
    g_hi, g_lo = _split_bf16(g)
    gt_hi, gt_lo = _split_bf16(gt)
    cum_col = _dot(tri, g_hi) + _dot(tri, g_lo)
    cum_row = _dot_nt(gt_hi, tri) + _dot_nt(gt_lo, tri)
    ones = jnp.ones((CHUNK, DH), BF16)
    for h in range(N_HEADS):
        ci = d * 8 + h
        cf = d * 8 + 4 + h
        sl = slice(h * DH, (h + 1) * DH)
        q = q_ref[:, sl].astype(BF16)
        k = k_ref[:, sl]
        v_ext = jnp.concatenate([v_ref[:, sl].astype(BF16), ones], axis=1)
        b_col = cum_col[:, cf:cf + 1]
        i_col = g[:, ci:ci + 1]
        b_row = cum_row[cf:cf + 1, :]
        i_row = gt[ci:ci + 1, :]
        m_prev = m_ref[d * 4 + h:d * 4 + h + 1, 0:1]
        dmat = jnp.where(mask, b_col - b_row + i_row, NEG_INF)
        prior = b_col + m_prev
        m_t = jnp.maximum(prior, jnp.max(dmat, axis=1, keepdims=True))
        w_intra = jnp.exp(dmat - m_t)
        w_prior = jnp.exp(prior - m_t)
        s = _dot_nt(q, k.astype(BF16)) * w_intra
        cext = cext_ref[d, h]
        tot = w_prior * _dot(q, cext.astype(BF16)) + _dot(s.astype(BF16), v_ext)
        num = tot[:, :DH]
        den = tot[:, DH:]
        h_ref[:, sl] = num / jnp.maximum(jnp.abs(den), jnp.exp(-m_t))
        b_last = b_col[CHUNK - 1:CHUNK, :] if d == 0 else b_col[0:1, :]
        log_end = b_last - b_col + i_col
        m_new = jnp.maximum(b_last + m_prev, jnp.max(log_end, axis=0, keepdims=True))
        w_end = jnp.exp(log_end - m_new)
        decay = jnp.exp(b_last + m_prev - m_new)
        cext_ref[d, h] = decay * cext + _dot_tn((k * w_end).astype(BF16), v_ext)
        m_ref[d * 4 + h:d * 4 + h + 1, :] = jnp.broadcast_to(m_new, (1, DH))


def _mlstm_kernel(qf_ref, kf_ref, vf_ref, gf_ref, gtf_ref, qb_ref, kb_ref, vb_ref, gb_ref, gtb_ref,
                  c0_ref, n0_ref, m0_ref, hf_ref, hb_ref, cn_ref, nn_ref, mn_ref, cext_ref, m_ref):
    c = pl.program_id(1)

    @pl.when(c == 0)
    def _init():
        for d in range(2):
            for h in range(N_HEADS):
                cext_ref[d, h, :, 0:DH] = c0_ref[d, h]
                cext_ref[d, h, :, DH:2 * DH] = jnp.broadcast_to(n0_ref[d, h], (DH, DH))
        m_ref[...] = m0_ref[...]

    _mlstm_direction(0, qf_ref, kf_ref, vf_ref, gf_ref, gtf_ref, hf_ref, cext_ref, m_ref)
    _mlstm_direction(1, qb_ref, kb_ref, vb_ref, gb_ref, gtb_ref, hb_ref, cext_ref, m_ref)

    @pl.when(c == pl.num_programs(1) - 1)
    def _final():
        for d in range(2):
            for h in range(N_HEADS):
                cn_ref[d, h] = cext_ref[d, h, :, 0:DH]
                nn_ref[d, h] = cext_ref[d, h, :, DH:DH + 1]
        mn_ref[...] = m_ref[...]


def _mlstm(qk, y, g, gt, c0, n0, m0, n_seq, seq_len, tok0):
    nc = seq_len // CHUNK
    blk0 = tok0 // CHUNK
    vcol = COL_V // D_MLSTM

    def chunk_specs(chunk_of):
        tok = lambda b, c: blk0 + b * nc + chunk_of(c)
        return [
            pl.BlockSpec((CHUNK, D_MLSTM), lambda b, c: (tok(b, c), 0)),
            pl.BlockSpec((CHUNK, D_MLSTM), lambda b, c: (tok(b, c), 1)),
            pl.BlockSpec((CHUNK, D_MLSTM), lambda b, c: (tok(b, c), vcol)),
            pl.BlockSpec((CHUNK, GATE_PAD), lambda b, c: (tok(b, c), 0)),
            pl.BlockSpec((N_GATES, CHUNK), lambda b, c: (0, tok(b, c))),
        ]

    fwd_chunk = lambda c: c
    bwd_chunk = lambda c: nc - 1 - c
    state_specs = [
        pl.BlockSpec((None, 2, N_HEADS, DH, DH), lambda b, c: (b, 0, 0, 0, 0)),
        pl.BlockSpec((None, 2, N_HEADS, DH, 1), lambda b, c: (b, 0, 0, 0, 0)),
        pl.BlockSpec((None, 2 * N_HEADS, DH), lambda b, c: (b, 0, 0)),
    ]
    n_tok = n_seq * seq_len
    return pl.pallas_call(
        _mlstm_kernel,
        out_shape=(
            jax.ShapeDtypeStruct((n_tok, D_MLSTM), F32),
            jax.ShapeDtypeStruct((n_tok, D_MLSTM), F32),
            jax.ShapeDtypeStruct((n_seq, 2, N_HEADS, DH, DH), F32),
            jax.ShapeDtypeStruct((n_seq, 2, N_HEADS, DH, 1), F32),
            jax.ShapeDtypeStruct((n_seq, 2 * N_HEADS, DH), F32),
        ),
        grid=(n_seq, nc),
        in_specs=chunk_specs(fwd_chunk) + chunk_specs(bwd_chunk) + state_specs,
        out_specs=(
            pl.BlockSpec((CHUNK, D_MLSTM), lambda b, c: (b * nc + fwd_chunk(c), 0)),
            pl.BlockSpec((CHUNK, D_MLSTM), lambda b, c: (b * nc + bwd_chunk(c), 0)),
            state_specs[0], state_specs[1], state_specs[2],
        ),
        scratch_shapes=[pltpu.VMEM((2, N_HEADS, DH, 2 * DH), F32), pltpu.VMEM((2 * N_HEADS, DH), F32)],
        compiler_params=_params("arbitrary", "arbitrary"),
        name="mlstm_scan",
    )(qk, qk, y, g, gt, qk, qk, y, g, gt, c0, n0, m0)


MIX_TM = 512


def _mix_kernel(x_ref, u_ref, vs_ref, o_ref, hf_ref, hb_ref, mod_ref, ws_ref, bs_ref, nw_ref,
                wout_ref, xo_ref, h2_ref, h2t_ref, cat_ref):
    for c0 in range(0, MIX_TM, CHUNK):
        rows = slice(c0, c0 + CHUNK)
        for h in range(N_HEADS):
            sl = slice(h * DH, (h + 1) * DH)
            vh = _rms(vs_ref[rows, sl]).astype(BF16)
            mixed = _dot(ws_ref[h], vh) + bs_ref[h]
            cat_ref[rows, sl] = (u_ref[rows, sl] * mixed).astype(BF16)
            hm = _rms(hf_ref[rows, sl] + hb_ref[rows, sl]) * nw_ref[:, sl]
            cat_ref[rows, D_GMLP + h * DH:D_GMLP + (h + 1) * DH] = (
                hm * _sigmoid(o_ref[rows, sl])).astype(BF16)
    mix = _dot(cat_ref[...], wout_ref[...])
    x_new = x_ref[...] + mod_ref[2:3, :] * mix
    xo_ref[...] = x_new
    h2 = (_rms(x_new) * (1.0 + mod_ref[4:5, :]) + mod_ref[3:4, :]).astype(BF16)
    h2_ref[...] = h2
    h2t_ref[...] = h2.T


def _mix(x, y, hf, hb, mod_l, ws, bs_b, nw, wout):
    blk = lambda col: (lambda i: (i, col))
    return pl.pallas_call(
        _mix_kernel,
        out_shape=(
            jax.ShapeDtypeStruct((T_ALL, D_MODEL), F32),
            jax.ShapeDtypeStruct((T_ALL, D_MODEL), BF16),
            jax.ShapeDtypeStruct((D_MODEL, T_ALL), BF16),
        ),
        grid=(T_ALL // MIX_TM,),
        in_specs=[
            pl.BlockSpec((MIX_TM, D_MODEL), blk(0)),
            pl.BlockSpec((MIX_TM, D_GMLP), blk(COL_U // D_GMLP)),
            pl.BlockSpec((MIX_TM, D_GMLP), blk(COL_VS // D_GMLP)),
            pl.BlockSpec((MIX_TM, D_MLSTM), blk(COL_O // D_MLSTM)),
            pl.BlockSpec((MIX_TM, D_MLSTM), blk(0)),
            pl.BlockSpec((MIX_TM, D_MLSTM), blk(0)),
            pl.BlockSpec((None, 6, D_MODEL), lambda i: (_segment(i, MIX_TM), 0, 0)),
            pl.BlockSpec((N_HEADS, CHUNK, CHUNK), lambda i: (0, 0, 0)),
            pl.BlockSpec((N_HEADS, CHUNK, DH), lambda i: (0, 0, 0)),
            pl.BlockSpec((1, D_MLSTM), lambda i: (0, 0)),
            pl.BlockSpec((D_MODEL, D_MODEL), lambda i: (0, 0)),
        ],
        out_specs=(
            pl.BlockSpec((MIX_TM, D_MODEL), blk(0)),
            pl.BlockSpec((MIX_TM, D_MODEL), blk(0)),
            pl.BlockSpec((D_MODEL, MIX_TM), lambda i: (0, i)),
        ),
        scratch_shapes=[pltpu.VMEM((MIX_TM, D_MODEL), BF16)],
        compiler_params=_params("arbitrary"),
        name="mixer_out",
    )(x, y, y, y, hf, hb, mod_l, ws, bs_b, nw, wout)


SEL_TM = 512
_PAIR_COLS = [PEER_TOPK // (i + 1) for i in range(PEER_TOPK)]


def _top16(s):
    vals = []
    for _ in range(PEER_TOPK):
        m = jnp.max(s, axis=0, keepdims=True)
        vals.append(m)
        s = jnp.where(s == m, NEG_INF, s)
    return vals


def _select_chunk(s1, s2):
    v1 = _top16(s1)
    v2 = _top16(s2)
    v2_all = jnp.concatenate(v2, axis=0)
    v1_tail = jnp.concatenate(v1[8:], axis=0)
    jidx = lax.broadcasted_iota(jnp.int32, (8, s1.shape[1]), 0)
    cands = [v1[0] + v2_all]
    for i in range(1, 8):
        cands.append(jnp.where(jidx < _PAIR_COLS[i], v1[i] + v2_all[0:8], NEG_INF))
    cands.append(v1_tail + v2[0])
    cur = list(cands)
    tau = None
    for _ in range(PEER_TOPK):
        m = jnp.max(cur[0], axis=0, keepdims=True)
        for c in cur[1:]:
            m = jnp.maximum(m, jnp.max(c, axis=0, keepdims=True))
        tau = m
        cur = [jnp.where(c == m, NEG_INF, c) for c in cur]
    best = v1[0] + v2[0]
    z = None
    thr_rank = []
    for idx, c in enumerate(cands):
        sel = c >= tau
        e = jnp.sum(jnp.where(sel, jnp.exp(c - best), 0.0), axis=0, keepdims=True)
        z = e if z is None else z + e
        if idx == 0:
            thr_rank.append(jnp.min(jnp.where(sel, v2_all, POS_INF), axis=0, keepdims=True))
        elif idx < 8:
            thr_rank.append(jnp.min(jnp.where(sel, v2_all[0:8], POS_INF), axis=0, keepdims=True))
        else:
            tail = jnp.where(sel, v2[0], POS_INF)
            thr_rank.extend(tail[r:r + 1] for r in range(8))
    thr = jnp.full(s1.shape, POS_INF, F32)
    for i in range(PEER_TOPK):
        thr = jnp.where(s1 == v1[i], thr_rank[i], thr)
    p1 = jnp.exp(s1 - v1[0]) / z
    p2 = jnp.exp(s2 - v2[0])
    return p2, thr, p1


def _peer_sel_kernel(h_ref, wq_ref, keys_ref, s2_ref, p2_ref, thr_ref, p1_ref, sc_ref):
    q = _dot(h_ref[...], wq_ref[...])
    for p in range(2):
        sc_ref[p] = _dot_nt(keys_ref[p], q[:, p * 128:(p + 1) * 128].astype(BF16))

    def chunk(j, carry):
        lanes = pl.ds(pl.multiple_of(j * 128, 128), 128)
        s1 = sc_ref[0, :, lanes]
        s2 = sc_ref[1, :, lanes]
        p2, thr, p1 = _select_chunk(s1, s2)
        s2_ref[:, lanes] = s2
        p2_ref[:, lanes] = p2
        thr_ref[:, lanes] = thr
        p1_ref[:, lanes] = p1
        return carry

    lax.fori_loop(0, SEL_TM // 128, chunk, 0)


def _peer_select(h2, wq, keys):
    tab = jax.ShapeDtypeStruct((PEER_HEADS, N_KEYS, T_ALL), F32)
    tab_spec = pl.BlockSpec((None, N_KEYS, SEL_TM), lambda i, h: (h, 0, i))
    return pl.pallas_call(
        _peer_sel_kernel,
        out_shape=(tab, tab, tab, tab),
        grid=(T_ALL // SEL_TM, PEER_HEADS),
        in_specs=[
            pl.BlockSpec((SEL_TM, D_MODEL), lambda i, h: (i, 0)),
            pl.BlockSpec((D_MODEL, 2 * 128), lambda i, h: (0, h)),
            pl.BlockSpec((None, 2, N_KEYS, 128), lambda i, h: (h, 0, 0, 0)),
        ],
        out_specs=(tab_spec, tab_spec, tab_spec, tab_spec),
        scratch_shapes=[pltpu.VMEM((2, N_KEYS, SEL_TM), F32)],
        compiler_params=_params("arbitrary", "arbitrary"),
        name="peer_select",
    )(h2, wq, keys)


EXP_TM = 512
EXP_KEY_ROWS = 8
EXP_TE = EXP_KEY_ROWS * N_KEYS


def _peer_expert_kernel(ht_ref, u_ref, vt_ref, s2_ref, p2_ref, thr_ref, p1_ref, x_ref, mod_ref,
                        o_ref, act_ref, gate_ref, acc_ref):
    j = pl.program_id(1)

    @pl.when(j == 0)
    def _zero():
        acc_ref[...] = jnp.zeros_like(acc_ref)

    act_ref[...] = _dot(u_ref[...], ht_ref[...])

    def lane_chunk(lc, carry):
        lanes = pl.ds(pl.multiple_of(lc * 128, 128), 128)
        for r in range(EXP_KEY_ROWS):
            erow = slice(r * N_KEYS, (r + 1) * N_KEYS)
            w = jnp.zeros((N_KEYS, 128), F32)
            for h in range(PEER_HEADS):
                thr = thr_ref[h, r:r + 1, lanes]
                p1 = p1_ref[h, r:r + 1, lanes]
                w = w + jnp.where(s2_ref[h, :, lanes] >= thr, p2_ref[h, :, lanes], 0.0) * p1
            gate_ref[erow, lanes] = (w * _gelu_tanh(act_ref[erow, lanes])).astype(BF16)
        return carry

    lax.fori_loop(0, EXP_TM // 128, lane_chunk, 0)
    acc_ref[...] += _dot(vt_ref[...], gate_ref[...])

    @pl.when(j == pl.num_programs(1) - 1)
    def _out():
        o_ref[...] = x_ref[...] + mod_ref[5:6, :] * acc_ref[...].T


def _peer_experts(h2t, u, vt, s2, p2, thr, p1, x, mod_l):
    tab_spec = pl.BlockSpec((PEER_HEADS, N_KEYS, EXP_TM), lambda i, j: (0, 0, i))
    row_spec = pl.BlockSpec((PEER_HEADS, EXP_KEY_ROWS, EXP_TM), lambda i, j: (0, j, i))
    return pl.pallas_call(
        _peer_expert_kernel,
        out_shape=jax.ShapeDtypeStruct((T_ALL, D_MODEL), F32),
        grid=(T_ALL // EXP_TM, N_EXPERTS // EXP_TE),
        in_specs=[
            pl.BlockSpec((D_MODEL, EXP_TM), lambda i, j: (0, i)),
            pl.BlockSpec((EXP_TE, D_MODEL), lambda i, j: (j, 0)),
            pl.BlockSpec((D_MODEL, EXP_TE), lambda i, j: (0, j)),
            tab_spec, tab_spec, row_spec, row_spec,
            pl.BlockSpec((EXP_TM, D_MODEL), lambda i, j: (i, 0)),
            pl.BlockSpec((None, 6, D_MODEL), lambda i, j: (_segment(i, EXP_TM), 0, 0)),
        ],
        out_specs=pl.BlockSpec((EXP_TM, D_MODEL), lambda i, j: (i, 0)),
        scratch_shapes=[
            pltpu.VMEM((EXP_TE, EXP_TM), F32),
            pltpu.VMEM((EXP_TE, EXP_TM), BF16),
            pltpu.VMEM((D_MODEL, EXP_TM), F32),
        ],
        compiler_params=_params("arbitrary", "arbitrary"),
        name="peer_experts",
    )(h2t, u, vt, s2, p2, thr, p1, x, mod_l)


FIN_TM = 1024


def _final_kernel(x_ref, w_ref, o_ref):
    o_ref[...] = _rms(x_ref[...]) * w_ref[...]


def _final_norm(x, w):
    return pl.pallas_call(
        _final_kernel,
        out_shape=jax.ShapeDtypeStruct((T_ALL, D_MODEL), F32),
        grid=(T_ALL // FIN_TM,),
        in_specs=[pl.BlockSpec((FIN_TM, D_MODEL), lambda i: (i, 0)),
                  pl.BlockSpec((1, D_MODEL), lambda i: (0, 0))],
        out_specs=pl.BlockSpec((FIN_TM, D_MODEL), lambda i: (i, 0)),
        compiler_params=_params("arbitrary"),
        name="final_norm",
    )(x, w)


def kernel(x_prompt, x_sample, state_C, state_n, state_m, c, c_ctx, w_mod, b_mod, w_in, b_gate,
           conv_w, w_s, b_s, mlstm_norm_w, w_out, peer_wq, peer_keys, peer_u, peer_v, final_norm_w):
    assert x_prompt.shape == (N_CTX_SEQ, CTX_LEN, D_MODEL)
    assert x_sample.shape == (N_LAT_SEQ, LAT_LEN, D_MODEL)
    x = jnp.concatenate([x_prompt.reshape(T_CTX, D_MODEL), x_sample.reshape(T_LAT, D_MODEL)], axis=0)

    cond = jnp.zeros((8, D_MODEL), F32).at[0].set(c_ctx).at[1:1 + N_LAT_SEQ].set(c)
    mod = _modulation(cond, w_mod, b_mod)[:, :1 + N_LAT_SEQ].reshape(DEPTH, 1 + N_LAT_SEQ, 6, D_MODEL)

    zero_c = jnp.zeros((N_CTX_SEQ, 2, N_HEADS, DH, DH), F32)
    zero_n = jnp.zeros((N_CTX_SEQ, 2, N_HEADS, DH, 1), F32)
    zero_m = jnp.zeros((N_CTX_SEQ, 2 * N_HEADS, DH), F32)

    new_c, new_n, new_m = [], [], []
    for l in range(DEPTH):
        w_main = w_in[l, :, :D_PROJ].astype(BF16)
        w_gate = w_in[l, :, D_PROJ:]
        wg = jnp.pad(w_gate, ((0, 0), (0, GATE_PAD - N_GATES))).astype(BF16)
        wgt = w_gate.T.astype(BF16)
        bg = jnp.pad(b_gate[l].reshape(1, N_GATES), ((0, 0), (0, GATE_PAD - N_GATES)))
        bgt = b_gate[l].reshape(N_GATES, 1)
        y, g, gt = _inproj(x, mod[l], w_main, wg, wgt, bg, bgt)

        qk = _conv(y, conv_w[l].reshape(9, 2 * D_MLSTM))

        hf_c, hb_c, c_ctx_new, n_ctx_new, m_ctx_new = _mlstm(
            qk, y, g, gt, zero_c, zero_n, zero_m, N_CTX_SEQ, CTX_LEN, 0)
        lat_m0 = jnp.broadcast_to(state_m[:, l].reshape(N_LAT_SEQ, 2 * N_HEADS, 1),
                                  (N_LAT_SEQ, 2 * N_HEADS, DH))
        hf_l, hb_l, _, _, _ = _mlstm(
            qk, y, g, gt, state_C[:, l], state_n[:, l][..., None], lat_m0, N_LAT_SEQ, LAT_LEN, T_CTX)
        hf = jnp.concatenate([hf_c, hf_l], axis=0)
        hb = jnp.concatenate([hb_c, hb_l], axis=0)
        new_c.append(c_ctx_new)
        new_n.append(n_ctx_new[..., 0])
        new_m.append(m_ctx_new[..., 0].reshape(N_CTX_SEQ, 2, N_HEADS))

        bs_b = jnp.broadcast_to(b_s[l][:, :, None], (N_HEADS, CHUNK, DH))
        x, h2, h2t = _mix(x, y, hf, hb, mod[l], w_s[l].astype(BF16), bs_b,
                          mlstm_norm_w[l].reshape(1, D_MLSTM), w_out[l].astype(BF16))

        s2, p2, thr, p1 = _peer_select(h2, peer_wq[l].astype(BF16), peer_keys[l].astype(BF16))
        x = _peer_experts(h2t, peer_u[l].astype(BF16), peer_v[l].T.astype(BF16),
                          s2, p2, thr, p1, x, mod[l])

    y_all = _final_norm(x, final_norm_w.reshape(1, D_MODEL))
    y_prompt = y_all[:T_CTX].reshape(N_CTX_SEQ, CTX_LEN, D_MODEL)
    y_sample = y_all[T_CTX:].reshape(N_LAT_SEQ, LAT_LEN, D_MODEL)
    return (y_prompt, y_sample, jnp.stack(new_c, axis=1), jnp.stack(new_n, axis=1),
            jnp.stack(new_m, axis=1))
```

```python
import functools

import jax
import jax.numpy as jnp
from jax import lax
from jax.experimental import pallas as pl
from jax.experimental.pallas import tpu as pltpu

F32 = jnp.float32
BF16 = jnp.bfloat16

D_MODEL = 1024
DEPTH = 4
N_CTX_SEQ, CTX_LEN = 16, 256
N_LAT_SEQ, LAT_LEN = 2, 2048
T_CTX = N_CTX_SEQ * CTX_LEN
T_LAT = N_LAT_SEQ * LAT_LEN
T_ALL = T_CTX + T_LAT
GRID_W = 64
GRID_H = LAT_LEN // GRID_W
CHUNK = 128
D_GMLP = 512
D_MLSTM = 512
N_HEADS = 4
DH = 128
N_KEYS = 128
N_EXPERTS = N_KEYS * N_KEYS
PEER_HEADS = 8
PEER_TOPK = 16
D_PROJ = 2 * D_GMLP + 4 * D_MLSTM
N_GATES = 16
EPS = 1e-6
NEG_INF = float("-inf")
POS_INF = float("inf")

VMEM_LIMIT_BYTES = 56 * 1024 * 1024

COL_U, COL_VS, COL_QK, COL_V, COL_O = 0, 512, 1024, 2048, 2560


def _dot(a, b):
    return jnp.dot(a, b, preferred_element_type=F32)


def _dot_nt(a, b):
    return lax.dot_general(a, b, (((1,), (1,)), ((), ())), preferred_element_type=F32)


def _dot_tn(a, b):
    return lax.dot_general(a, b, (((0,), (0,)), ((), ())), preferred_element_type=F32)


def _split_bf16(x):
    hi = x.astype(BF16)
    lo = (x - hi.astype(F32)).astype(BF16)
    return hi, lo


def _rms(x):
    return x * lax.rsqrt(jnp.mean(x * x, axis=-1, keepdims=True) + EPS)


def _sigmoid(x):
    return 1.0 / (1.0 + jnp.exp(-x))


def _gelu_tanh(x):
    return 0.5 * x * (1.0 + jnp.tanh(0.7978845608028654 * (x + 0.044715 * (x * x * x))))


def _segment(i, tile):
    start = i * tile
    return (start >= T_CTX).astype(jnp.int32) + (start >= T_CTX + LAT_LEN).astype(jnp.int32)


def _params(*sem):
    return pltpu.CompilerParams(dimension_semantics=sem, vmem_limit_bytes=VMEM_LIMIT_BYTES)


MOD_TN = 1536


def _mod_kernel(c_ref, w_ref, b_ref, o_ref):
    c = c_ref[...]
    a = (c * _sigmoid(c)).astype(BF16)
    o_ref[...] = _dot(a, w_ref[...].astype(BF16)) + b_ref[...]


def _modulation(cond, w_mod, b_mod):
    n = w_mod.shape[-1]
    return pl.pallas_call(
        _mod_kernel,
        out_shape=jax.ShapeDtypeStruct((DEPTH, 8, n), F32),
        grid=(DEPTH, n // MOD_TN),
        in_specs=[
            pl.BlockSpec((8, D_MODEL), lambda l, j: (0, 0)),
            pl.BlockSpec((None, D_MODEL, MOD_TN), lambda l, j: (l, 0, j)),
            pl.BlockSpec((None, 1, MOD_TN), lambda l, j: (l, 0, j)),
        ],
        out_specs=pl.BlockSpec((None, 8, MOD_TN), lambda l, j: (l, 0, j)),
        compiler_params=_params("arbitrary", "arbitrary"),
        name="modulation",
    )(cond, w_mod, b_mod.reshape(DEPTH, 1, n))


IN_TM = 512
IN_TN = 512
GATE_PAD = 128


def _inproj_kernel(x_ref, mod_ref, w_ref, wg_ref, wgt_ref, bg_ref, bgt_ref, y_ref, g_ref, gt_ref):
    x = x_ref[...]
    h = _rms(x) * (1.0 + mod_ref[1:2, :]) + mod_ref[0:1, :]
    hb = h.astype(BF16)
    for n0 in range(0, D_PROJ, IN_TN):
        y_ref[:, n0:n0 + IN_TN] = _dot(hb, w_ref[:, n0:n0 + IN_TN])

    def gate_act(g, idx):
        log_sig = jnp.minimum(g, 0.0) - jnp.log(1.0 + jnp.exp(-jnp.abs(g)))
        return jnp.where((idx & 4) != 0, log_sig, g)

    g = _dot(hb, wg_ref[...]) + bg_ref[...]
    g_ref[...] = gate_act(g, lax.broadcasted_iota(jnp.int32, g.shape, 1))
    gt = _dot_nt(wgt_ref[...], hb) + bgt_ref[...]
    gt_ref[...] = gate_act(gt, lax.broadcasted_iota(jnp.int32, gt.shape, 0))


def _inproj(x, mod_l, w_main, wg, wgt, bg, bgt):
    return pl.pallas_call(
        _inproj_kernel,
        out_shape=(
            jax.ShapeDtypeStruct((T_ALL, D_PROJ), F32),
            jax.ShapeDtypeStruct((T_ALL, GATE_PAD), F32),
            jax.ShapeDtypeStruct((N_GATES, T_ALL), F32),
        ),
        grid=(T_ALL // IN_TM,),
        in_specs=[
            pl.BlockSpec((IN_TM, D_MODEL), lambda i: (i, 0)),
            pl.BlockSpec((None, 6, D_MODEL), lambda i: (_segment(i, IN_TM), 0, 0)),
            pl.BlockSpec((D_MODEL, D_PROJ), lambda i: (0, 0)),
            pl.BlockSpec((D_MODEL, GATE_PAD), lambda i: (0, 0)),
            pl.BlockSpec((N_GATES, D_MODEL), lambda i: (0, 0)),
            pl.BlockSpec((1, GATE_PAD), lambda i: (0, 0)),
            pl.BlockSpec((N_GATES, 1), lambda i: (0, 0)),
        ],
        out_specs=(
            pl.BlockSpec((IN_TM, D_PROJ), lambda i: (i, 0)),
            pl.BlockSpec((IN_TM, GATE_PAD), lambda i: (i, 0)),
            pl.BlockSpec((N_GATES, IN_TM), lambda i: (0, i)),
        ),
        compiler_params=_params("arbitrary"),
        name="inproj",
    )(x, mod_l, w_main, wg, wgt, bg, bgt)


CONV_CB = 128
CONV_ROWS = 4096


def _conv_taps(x, w_ref, taps):
    rows = x.shape[0]
    acc = None
    for w_row, delta, valid in taps:
        shifted = x if delta == 0 else pltpu.roll(x, (-delta) % rows, 0)
        term = shifted * w_ref[w_row:w_row + 1, :]
        if valid is not None:
            term = jnp.where(valid, term, 0.0)
        acc = term if acc is None else acc + term
    return acc


def _conv_kernel(x_ref, w_ref, o_ref):
    grp = pl.program_id(0)
    cblk = pl.program_id(1)
    scale = jnp.where(cblk >= (D_MLSTM // CONV_CB), DH ** -0.5, 1.0).astype(F32)

    def finish(y):
        o_ref[...] = (y * _sigmoid(y)) * scale

    @pl.when(grp == 0)
    def _ctx():
        x = x_ref[...]
        pos = lax.broadcasted_iota(jnp.int32, x.shape, 0) & (CTX_LEN - 1)
        taps = [(3, -1, pos >= 1), (4, 0, None), (5, 1, pos <= CTX_LEN - 2)]
        finish(_conv_taps(x, w_ref, taps))

    @pl.when(grp == 1)
    def _lat():
        x = x_ref[...]
        pos = lax.broadcasted_iota(jnp.int32, x.shape, 0) & (LAT_LEN - 1)
        r = pos >> 6
        c = pos & (GRID_W - 1)
        row_ok = {-1: r >= 1, 0: None, 1: r <= GRID_H - 2}
        col_ok = {-1: c >= 1, 0: None, 1: c <= GRID_W - 2}
        taps = []
        for di in (-1, 0, 1):
            for dj in (-1, 0, 1):
                ok = row_ok[di]
                if col_ok[dj] is not None:
                    ok = col_ok[dj] if ok is None else (ok & col_ok[dj])
                taps.append(((di + 1) * 3 + (dj + 1), di * GRID_W + dj, ok))
        finish(_conv_taps(x, w_ref, taps))


def _conv(y, conv_w9):
    qk_blk0 = COL_QK // CONV_CB
    return pl.pallas_call(
        _conv_kernel,
        out_shape=jax.ShapeDtypeStruct((T_ALL, 2 * D_MLSTM), F32),
        grid=(T_ALL // CONV_ROWS, 2 * D_MLSTM // CONV_CB),
        in_specs=[
            pl.BlockSpec((CONV_ROWS, CONV_CB), lambda g, j: (g, qk_blk0 + j)),
            pl.BlockSpec((9, CONV_CB), lambda g, j: (0, j)),
        ],
        out_specs=pl.BlockSpec((CONV_ROWS, CONV_CB), lambda g, j: (g, j)),
        compiler_params=_params("arbitrary", "arbitrary"),
        name="conv_silu",
    )(y, conv_w9)


def _mlstm_direction(d, q_ref, k_ref, v_ref, g_ref, gt_ref, h_ref, cext_ref, m_ref):
    row = lax.broadcasted_iota(jnp.int32, (CHUNK, CHUNK), 0)
    col = lax.broadcasted_iota(jnp.int32, (CHUNK, CHUNK), 1)
    mask = (col <= row) if d == 0 else (col >= row)
    tri = jnp.where(mask, 1.0, 0.0).astype(BF16)
    g = g_ref[...]
    gt = gt_ref[...]
    g_hi, g_lo = _split_bf16(g)
    gt_hi, gt_lo = _split_bf16(gt)
    cum_col = _dot(tri, g_hi) + _dot(tri, g_lo)
    cum_row = _dot_nt(gt_hi, tri) + _dot_nt(gt_lo, tri)
    ones = jnp.ones((CHUNK, DH), BF16)
    for h in range(N_HEADS):
        ci = d * 8 + h
        cf = d * 8 + 4 + h
        sl = slice(h * DH, (h + 1) * DH)
        q = q_ref[:, sl].astype(BF16)
        k = k_ref[:, sl]
        v_ext = jnp.concatenate([v_ref[:, sl].astype(BF16), ones], axis=1)
        b_col = cum_col[:, cf:cf + 1]
        i_col = g[:, ci:ci + 1]
        b_row = cum_row[cf:cf + 1, :]
        i_row = gt[ci:ci + 1, :]
        m_prev = m_ref[d * 4 + h:d * 4 + h + 1, 0:1]
        dmat = jnp.where(mask, b_col - b_row + i_row, NEG_INF)
        prior = b_col + m_prev
        m_t = jnp.maximum(prior, jnp.max(dmat, axis=1, keepdims=True))
        w_intra = jnp.exp(dmat - m_t)
        w_prior = jnp.exp(prior - m_t)
        s = _dot_nt(q, k.astype(BF16)) * w_intra
        cext = cext_ref[d, h]
        tot = w_prior * _dot(q, cext.astype(BF16)) + _dot(s.astype(BF16), v_ext)
        num = tot[:, :DH]
        den = tot[:, DH:]
        h_ref[:, sl] = num / jnp.maximum(jnp.abs(den), jnp.exp(-m_t))
        b_last = b_col[CHUNK - 1:CHUNK, :] if d == 0 else b_col[0:1, :]
        log_end = b_last - b_col + i_col
        m_new = jnp.maximum(b_last + m_prev, jnp.max(log_end, axis=0, keepdims=True))
        w_end = jnp.exp(log_end - m_new)
        decay = jnp.exp(b_last + m_prev - m_new)
        cext_ref[d, h] = decay * cext + _dot_tn((k * w_end).astype(BF16), v_ext)
        m_ref[d * 4 + h:d * 4 + h + 1, :] = jnp.broadcast_to(m_new, (1, DH))


def _mlstm_kernel(qf_ref, kf_ref, vf_ref, gf_ref, gtf_ref, qb_ref, kb_ref, vb_ref, gb_ref, gtb_ref,
                  c0_ref, n0_ref, m0_ref, hf_ref, hb_ref, cn_ref, nn_ref, mn_ref, cext_ref, m_ref):
    c = pl.program_id(1)

    @pl.when(c == 0)
    def _init():
        for d in range(2):
            for h in range(N_HEADS):
                cext_ref[d, h, :, 0:DH] = c0_ref[d, h]
                cext_ref[d, h, :, DH:2 * DH] = jnp.broadcast_to(n0_ref[d, h], (DH, DH))
        m_ref[...] = m0_ref[...]

    _mlstm_direction(0, qf_ref, kf_ref, vf_ref, gf_ref, gtf_ref, hf_ref, cext_ref, m_ref)
    _mlstm_direction(1, qb_ref, kb_ref, vb_ref, gb_ref, gtb_ref, hb_ref, cext_ref, m_ref)

    @pl.when(c == pl.num_programs(1) - 1)
    def _final():
        for d in range(2):
            for h in range(N_HEADS):
                cn_ref[d, h] = cext_ref[d, h, :, 0:DH]
                nn_ref[d, h] = cext_ref[d, h, :, DH:DH + 1]
        mn_ref[...] = m_ref[...]


def _mlstm(qk, y, g, gt, c0, n0, m0, n_seq, seq_len, tok0):
    nc = seq_len // CHUNK
    blk0 = tok0 // CHUNK
    vcol = COL_V // D_MLSTM

    def chunk_specs(chunk_of):
        tok = lambda b, c: blk0 + b * nc + chunk_of(c)
        return [
            pl.BlockSpec((CHUNK, D_MLSTM), lambda b, c: (tok(b, c), 0)),
            pl.BlockSpec((CHUNK, D_MLSTM), lambda b, c: (tok(b, c), 1)),
            pl.BlockSpec((CHUNK, D_MLSTM), lambda b, c: (tok(b, c), vcol)),
            pl.BlockSpec((CHUNK, GATE_PAD), lambda b, c: (tok(b, c), 0)),
            pl.BlockSpec((N_GATES, CHUNK), lambda b, c: (0, tok(b, c))),
        ]

    fwd_chunk = lambda c: c
    bwd_chunk = lambda c: nc - 1 - c
    state_specs = [
        pl.BlockSpec((None, 2, N_HEADS, DH, DH), lambda b, c: (b, 0, 0, 0, 0)),
        pl.BlockSpec((None, 2, N_HEADS, DH, 1), lambda b, c: (b, 0, 0, 0, 0)),
        pl.BlockSpec((None, 2 * N_HEADS, DH), lambda b, c: (b, 0, 0)),
    ]
    n_tok = n_seq * seq_len
    return pl.pallas_call(
        _mlstm_kernel,
        out_shape=(
            jax.ShapeDtypeStruct((n_tok, D_MLSTM), F32),
            jax.ShapeDtypeStruct((n_tok, D_MLSTM), F32),
            jax.ShapeDtypeStruct((n_seq, 2, N_HEADS, DH, DH), F32),
            jax.ShapeDtypeStruct((n_seq, 2, N_HEADS, DH, 1), F32),
            jax.ShapeDtypeStruct((n_seq, 2 * N_HEADS, DH), F32),
        ),
        grid=(n_seq, nc),
        in_specs=chunk_specs(fwd_chunk) + chunk_specs(bwd_chunk) + state_specs,
        out_specs=(
            pl.BlockSpec((CHUNK, D_MLSTM), lambda b, c: (b * nc + fwd_chunk(c), 0)),
            pl.BlockSpec((CHUNK, D_MLSTM), lambda b, c: (b * nc + bwd_chunk(c), 0)),
            state_specs[0], state_specs[1], state_specs[2],
        ),
        scratch_shapes=[pltpu.VMEM((2, N_HEADS, DH, 2 * DH), F32), pltpu.VMEM((2 * N_HEADS, DH), F32)],
        compiler_params=_params("arbitrary", "arbitrary"),
        name="mlstm_scan",
    )(qk, qk, y, g, gt, qk, qk, y, g, gt, c0, n0, m0)


MIX_TM = 512


def _mix_kernel(x_ref, u_ref, vs_ref, o_ref, hf_ref, hb_ref, mod_ref, ws_ref, bs_ref, nw_ref,
                wout_ref, xo_ref, h2_ref, h2t_ref, cat_ref):
    for c0 in range(0, MIX_TM, CHUNK):
        rows = slice(c0, c0 + CHUNK)
        for h in range(N_HEADS):
            sl = slice(h * DH, (h + 1) * DH)
            vh = _rms(vs_ref[rows, sl]).astype(BF16)
            mixed = _dot(ws_ref[h], vh) + bs_ref[h]
            cat_ref[rows, sl] = (u_ref[rows, sl] * mixed).astype(BF16)
            hm = _rms(hf_ref[rows, sl] + hb_ref[rows, sl]) * nw_ref[:, sl]
            cat_ref[rows, D_GMLP + h * DH:D_GMLP + (h + 1) * DH] = (
                hm * _sigmoid(o_ref[rows, sl])).astype(BF16)
    mix = _dot(cat_ref[...], wout_ref[...])
    x_new = x_ref[...] + mod_ref[2:3, :] * mix
    xo_ref[...] = x_new
    h2 = (_rms(x_new) * (1.0 + mod_ref[4:5, :]) + mod_ref[3:4, :]).astype(BF16)
    h2_ref[...] = h2
    h2t_ref[...] = h2.T


def _mix(x, y, hf, hb, mod_l, ws, bs_b, nw, wout):
    blk = lambda col: (lambda i: (i, col))
    return pl.pallas_call(
        _mix_kernel,
        out_shape=(
            jax.ShapeDtypeStruct((T_ALL, D_MODEL), F32),
            jax.ShapeDtypeStruct((T_ALL, D_MODEL), BF16),
            jax.ShapeDtypeStruct((D_MODEL, T_ALL), BF16),
        ),
        grid=(T_ALL // MIX_TM,),
        in_specs=[
            pl.BlockSpec((MIX_TM, D_MODEL), blk(0)),
            pl.BlockSpec((MIX_TM, D_GMLP), blk(COL_U // D_GMLP)),
            pl.BlockSpec((MIX_TM, D_GMLP), blk(COL_VS // D_GMLP)),
            pl.BlockSpec((MIX_TM, D_MLSTM), blk(COL_O // D_MLSTM)),
            pl.BlockSpec((MIX_TM, D_MLSTM), blk(0)),
            pl.BlockSpec((MIX_TM, D_MLSTM), blk(0)),
            pl.BlockSpec((None, 6, D_MODEL), lambda i: (_segment(i, MIX_TM), 0, 0)),
            pl.BlockSpec((N_HEADS, CHUNK, CHUNK), lambda i: (0, 0, 0)),
            pl.BlockSpec((N_HEADS, CHUNK, DH), lambda i: (0, 0, 0)),
            pl.BlockSpec((1, D_MLSTM), lambda i: (0, 0)),
            pl.BlockSpec((D_MODEL, D_MODEL), lambda i: (0, 0)),
        ],
        out_specs=(
            pl.BlockSpec((MIX_TM, D_MODEL), blk(0)),
            pl.BlockSpec((MIX_TM, D_MODEL), blk(0)),
            pl.BlockSpec((D_MODEL, MIX_TM), lambda i: (0, i)),
        ),
        scratch_shapes=[pltpu.VMEM((MIX_TM, D_MODEL), BF16)],
        compiler_params=_params("arbitrary"),
        name="mixer_out",
    )(x, y, y, y, hf, hb, mod_l, ws, bs_b, nw, wout)


SEL_TM = 512
_PAIR_COLS = [PEER_TOPK // (i + 1) for i in range(PEER_TOPK)]


LANES = 128
SUBLANES = 8


def _batcher_pairs(n):
    pairs = []
    p = 1
    while p < n:
        k = p
        while k >= 1:
            for j in range(k % p, n - k, 2 * k):
                for i in range(min(k, n - j - k)):
                    if (i + j) // (2 * p) == (i + j + k) // (2 * p):
                        pairs.append((i + j, i + j + k))
            k //= 2
        p *= 2
    return pairs


_SORT16 = tuple(_batcher_pairs(PEER_TOPK))


def _compare_exchange(v, a, b):
    x, y = v[a], v[b]
    if y is None:
        return
    if x is None:
        v[a], v[b] = y, None
        return
    v[a], v[b] = jnp.maximum(x, y), jnp.minimum(x, y)


def _top16_sorted(tiles):
    v = list(tiles) + [None] * (PEER_TOPK - len(tiles))
    for a, b in _SORT16:
        _compare_exchange(v, a, b)
    for shift in (4, 2, 1):
        other = [None if x is None else pltpu.roll(x, shift, 0) for x in v]
        merged = []
        for i in range(PEER_TOPK):
            x, y = v[i], other[PEER_TOPK - 1 - i]
            merged.append(y if x is None else x if y is None else jnp.maximum(x, y))
        v = merged
        dist = PEER_TOPK // 2
        while dist >= 1:
            for i in range(PEER_TOPK):
                if (i & dist) == 0:
                    _compare_exchange(v, i, i + dist)
            dist //= 2
    return v


def _rows_to_tile(rows, sub):
    out = rows[0]
    for r in range(1, SUBLANES):
        out = jnp.where(sub == r, rows[r], out)
    return out


def _select_chunk(s1_tiles, s2_tiles):
    v1 = _top16_sorted(s1_tiles)
    v2 = _top16_sorted(s2_tiles)
    sub = lax.broadcasted_iota(jnp.int32, (SUBLANES, LANES), 0)
    v2_lo = _rows_to_tile(v2[0:8], sub)
    v2_hi = _rows_to_tile(v2[8:16], sub)
    v1_hi = _rows_to_tile(v1[8:16], sub)
    cands = [v1[0] + v2_lo, v1[0] + v2_hi]
    for i in range(1, 8):
        cands.append(jnp.where(sub < _PAIR_COLS[i], v1[i] + v2_lo, NEG_INF))
    cands.append(v1_hi + v2[0])
    tau = _top16_sorted(cands)[PEER_TOPK - 1]
    best = v1[0] + v2[0]
    z_tile = None
    thr_rank = []
    for idx, c in enumerate(cands[:9]):
        sel = c >= tau
        e = jnp.where(sel, jnp.exp(c - best), 0.0)
        z_tile = e if z_tile is None else z_tile + e
        low = jnp.where(sel, v2_hi if idx == 1 else v2_lo, POS_INF)
        if idx == 1:
            thr_rank[0] = jnp.minimum(thr_rank[0], low)
        else:
            thr_rank.append(low)
    z_tile = z_tile + jnp.where(cands[9] >= tau, jnp.exp(cands[9] - best), 0.0)
    inv_z = 1.0 / jnp.broadcast_to(jnp.sum(z_tile, axis=0, keepdims=True), (SUBLANES, LANES))
    thr_rank = [jnp.broadcast_to(jnp.min(t, axis=0, keepdims=True), (SUBLANES, LANES)) for t in thr_rank]
    p2, thr, p1 = [], [], []
    for s1, s2 in zip(s1_tiles, s2_tiles):
        t = jnp.where(s1 + v2[0] >= tau, v2[0], POS_INF)
        for i in range(8):
            t = jnp.where(s1 == v1[i], thr_rank[i], t)
        thr.append(t)
        p1.append(jnp.exp(s1 - v1[0]) * inv_z)
        p2.append(jnp.exp(s2 - v2[0]))
    return p2, thr, p1


def _peer_sel_kernel(h_ref, wq_ref, keys_ref, s2_ref, p2_ref, thr_ref, p1_ref, sc_ref):
    q = _dot(h_ref[...], wq_ref[...])
    for p in range(2):
        sc_ref[p] = _dot_nt(keys_ref[p], q[:, p * 128:(p + 1) * 128].astype(BF16))

    def chunk(j, carry):
        lanes = pl.ds(pl.multiple_of(j * LANES, LANES), LANES)
        tile = lambda p, k: sc_ref[p, k * SUBLANES:(k + 1) * SUBLANES, lanes]
        s1 = [tile(0, k) for k in range(N_KEYS // SUBLANES)]
        s2 = [tile(1, k) for k in range(N_KEYS // SUBLANES)]
        p2, thr, p1 = _select_chunk(s1, s2)
        for k in range(N_KEYS // SUBLANES):
            rows = slice(k * SUBLANES, (k + 1) * SUBLANES)
            s2_ref[j, rows, :] = s2[k]
            p2_ref[j, rows, :] = p2[k]
            thr_ref[j, rows, :] = thr[k]
            p1_ref[j, rows, :] = p1[k]
        return carry

    lax.fori_loop(0, SEL_TM // LANES, chunk, 0)


def _peer_select(h2, wq, keys):
    tab = jax.ShapeDtypeStruct((PEER_HEADS, T_ALL // LANES, N_KEYS, LANES), F32)
    tab_spec = pl.BlockSpec((None, SEL_TM // LANES, N_KEYS, LANES), lambda i, h: (h, i, 0, 0))
    return pl.pallas_call(
        _peer_sel_kernel,
        out_shape=(tab, tab, tab, tab),
        grid=(T_ALL // SEL_TM, PEER_HEADS),
        in_specs=[
            pl.BlockSpec((SEL_TM, D_MODEL), lambda i, h: (i, 0)),
            pl.BlockSpec((D_MODEL, 2 * 128), lambda i, h: (0, h)),
            pl.BlockSpec((None, 2, N_KEYS, 128), lambda i, h: (h, 0, 0, 0)),
        ],
        out_specs=(tab_spec, tab_spec, tab_spec, tab_spec),
        scratch_shapes=[pltpu.VMEM((2, N_KEYS, SEL_TM), F32)],
        compiler_params=_params("arbitrary", "arbitrary"),
        name="peer_select",
    )(h2, wq, keys)


EXP_TM = 512
EXP_KEY_ROWS = 8
EXP_TE = EXP_KEY_ROWS * N_KEYS
EXP_STEPS = N_EXPERTS // EXP_TE
EXP_CHUNK = 2 * N_KEYS
EXP_CHUNKS = EXP_TE // EXP_CHUNK


def _peer_expert_kernel(ht_ref, u_ref, unext_ref, vt_ref, s2_ref, p2_ref, thr_ref, p1_ref,
                        x_ref, mod_ref, o_ref, act_ref, gate_ref, acc_ref):
    j = pl.program_id(1)

    @pl.when(j == 0)
    def _head():
        acc_ref[...] = jnp.zeros_like(acc_ref)
        act_ref[0] = _dot(u_ref[0:EXP_CHUNK, :], ht_ref[...])

    part = None
    for c in range(EXP_CHUNKS):
        anchor = None
        for lc in range(EXP_TM // LANES):
            lanes = slice(lc * LANES, (lc + 1) * LANES)
            w = [None, None]
            for h in range(PEER_HEADS):
                s2 = s2_ref[h, lc]
                p2 = p2_ref[h, lc]
                for k in range(2):
                    r = 2 * c + k
                    term = jnp.where(s2 >= thr_ref[h, lc, r:r + 1, :], p2, 0.0) * p1_ref[h, lc, r:r + 1, :]
                    w[k] = term if w[k] is None else w[k] + term
            for k in range(2):
                rows = slice(k * N_KEYS, (k + 1) * N_KEYS)
                g = (w[k] * _gelu_tanh(act_ref[c, rows, lanes])).astype(BF16)
                gate_ref[c, rows, lanes] = g
                if anchor is None:
                    anchor = g[0:16, :]
        zero = pltpu.bitcast((pltpu.bitcast(anchor, jnp.uint32) >> 16) >> 16, BF16)
        nxt = unext_ref[...] if c == EXP_CHUNKS - 1 else u_ref[(c + 1) * EXP_CHUNK:(c + 2) * EXP_CHUNK, :]
        nxt = nxt + jnp.tile(zero, (EXP_CHUNK // 16, D_MODEL // LANES))
        act_ref[(c + 1) % EXP_CHUNKS] = _dot(nxt, ht_ref[...])
        d = _dot(vt_ref[:, c * EXP_CHUNK:(c + 1) * EXP_CHUNK], gate_ref[c])
        part = d if part is None else part + d
    acc_ref[...] += part

    @pl.when(j == EXP_STEPS - 1)
    def _tail():
        o_ref[...] = x_ref[...] + mod_ref[5:6, :] * acc_ref[...].T


def _peer_experts(h2t, u, vt, s2, p2, thr, p1, x, mod_l):
    n_chunks = N_EXPERTS // EXP_CHUNK
    tab_spec = pl.BlockSpec((PEER_HEADS, EXP_TM // LANES, N_KEYS, LANES), lambda i, j: (0, i, 0, 0))
    row_spec = pl.BlockSpec((PEER_HEADS, EXP_TM // LANES, EXP_KEY_ROWS, LANES), lambda i, j: (0, i, j, 0))
    return pl.pallas_call(
        _peer_expert_kernel,
        out_shape=jax.ShapeDtypeStruct((T_ALL, D_MODEL), F32),
        grid=(T_ALL // EXP_TM, EXP_STEPS),
        in_specs=[
            pl.BlockSpec((D_MODEL, EXP_TM), lambda i, j: (0, i)),
            pl.BlockSpec((EXP_TE, D_MODEL), lambda i, j: (j, 0)),
            pl.BlockSpec((EXP_CHUNK, D_MODEL),
                         lambda i, j: (jnp.minimum((j + 1) * EXP_CHUNKS, n_chunks - 1), 0)),
            pl.BlockSpec((D_MODEL, EXP_TE), lambda i, j: (0, j)),
            tab_spec, tab_spec, row_spec, row_spec,
            pl.BlockSpec((EXP_TM, D_MODEL), lambda i, j: (i, 0)),
            pl.BlockSpec((None, 6, D_MODEL), lambda i, j: (_segment(i, EXP_TM), 0, 0)),
        ],
        out_specs=pl.BlockSpec((EXP_TM, D_MODEL), lambda i, j: (i, 0)),
        scratch_shapes=[
            pltpu.VMEM((EXP_CHUNKS, EXP_CHUNK, EXP_TM), F32),
            pltpu.VMEM((EXP_CHUNKS, EXP_CHUNK, EXP_TM), BF16),
            pltpu.VMEM((D_MODEL, EXP_TM), F32),
        ],
        compiler_params=_params("arbitrary", "arbitrary"),
        name="peer_experts",
    )(h2t, u, u, vt, s2, p2, thr, p1, x, mod_l)


FIN_TM = 1024


def _final_kernel(x_ref, w_ref, o_ref):
    o_ref[...] = _rms(x_ref[...]) * w_ref[...]


def _final_norm(x, w):
    return pl.pallas_call(
        _final_kernel,
        out_shape=jax.ShapeDtypeStruct((T_ALL, D_MODEL), F32),
        grid=(T_ALL // FIN_TM,),
        in_specs=[pl.BlockSpec((FIN_TM, D_MODEL), lambda i: (i, 0)),
                  pl.BlockSpec((1, D_MODEL), lambda i: (0, 0))],
        out_specs=pl.BlockSpec((FIN_TM, D_MODEL), lambda i: (i, 0)),
        compiler_params=_params("arbitrary"),
        name="final_norm",
    )(x, w)


def kernel(x_prompt, x_sample, state_C, state_n, state_m, c, c_ctx, w_mod, b_mod, w_in, b_gate,
           conv_w, w_s, b_s, mlstm_norm_w, w_out, peer_wq, peer_keys, peer_u, peer_v, final_norm_w):
    assert x_prompt.shape == (N_CTX_SEQ, CTX_LEN, D_MODEL)
    assert x_sample.shape == (N_LAT_SEQ, LAT_LEN, D_MODEL)
    x = jnp.concatenate([x_prompt.reshape(T_CTX, D_MODEL), x_sample.reshape(T_LAT, D_MODEL)], axis=0)

    cond = jnp.zeros((8, D_MODEL), F32).at[0].set(c_ctx).at[1:1 + N_LAT_SEQ].set(c)
    mod = _modulation(cond, w_mod, b_mod)[:, :1 + N_LAT_SEQ].reshape(DEPTH, 1 + N_LAT_SEQ, 6, D_MODEL)

    zero_c = jnp.zeros((N_CTX_SEQ, 2, N_HEADS, DH, DH), F32)
    zero_n = jnp.zeros((N_CTX_SEQ, 2, N_HEADS, DH, 1), F32)
    zero_m = jnp.zeros((N_CTX_SEQ, 2 * N_HEADS, DH), F32)

    new_c, new_n, new_m = [], [], []
    for l in range(DEPTH):
        w_main = w_in[l, :, :D_PROJ].astype(BF16)
        w_gate = w_in[l, :, D_PROJ:]
        wg = jnp.pad(w_gate, ((0, 0), (0, GATE_PAD - N_GATES))).astype(BF16)
        wgt = w_gate.T.astype(BF16)
        bg = jnp.pad(b_gate[l].reshape(1, N_GATES), ((0, 0), (0, GATE_PAD - N_GATES)))
        bgt = b_gate[l].reshape(N_GATES, 1)
        y, g, gt = _inproj(x, mod[l], w_main, wg, wgt, bg, bgt)

        qk = _conv(y, conv_w[l].reshape(9, 2 * D_MLSTM))

        hf_c, hb_c, c_ctx_new, n_ctx_new, m_ctx_new = _mlstm(
            qk, y, g, gt, zero_c, zero_n, zero_m, N_CTX_SEQ, CTX_LEN, 0)
        lat_m0 = jnp.broadcast_to(state_m[:, l].reshape(N_LAT_SEQ, 2 * N_HEADS, 1),
                                  (N_LAT_SEQ, 2 * N_HEADS, DH))
        hf_l, hb_l, _, _, _ = _mlstm(
            qk, y, g, gt, state_C[:, l], state_n[:, l][..., None], lat_m0, N_LAT_SEQ, LAT_LEN, T_CTX)
        hf = jnp.concatenate([hf_c, hf_l], axis=0)
        hb = jnp.concatenate([hb_c, hb_l], axis=0)
        new_c.append(c_ctx_new)
        new_n.append(n_ctx_new[..., 0])
        new_m.append(m_ctx_new[..., 0].reshape(N_CTX_SEQ, 2, N_HEADS))

        bs_b = jnp.broadcast_to(b_s[l][:, :, None], (N_HEADS, CHUNK, DH))
        x, h2, h2t = _mix(x, y, hf, hb, mod[l], w_s[l].astype(BF16), bs_b,
                          mlstm_norm_w[l].reshape(1, D_MLSTM), w_out[l].astype(BF16))

        s2, p2, thr, p1 = _peer_select(h2, peer_wq[l].astype(BF16), peer_keys[l].astype(BF16))
        x = _peer_experts(h2t, peer_u[l].astype(BF16), peer_v[l].T.astype(BF16),
                          s2, p2, thr, p1, x, mod[l])

    y_all = _final_norm(x, final_norm_w.reshape(1, D_MODEL))
    y_prompt = y_all[:T_CTX].reshape(N_CTX_SEQ, CTX_LEN, D_MODEL)
    y_sample = y_all[T_CTX:].reshape(N_LAT_SEQ, LAT_LEN, D_MODEL)
    return (y_prompt, y_sample, jnp.stack(new_c, axis=1), jnp.stack(new_n, axis=1),
            jnp.stack(new_m, axis=1))
```

```python
import jax
import jax.numpy as jnp
from jax import lax
from jax.experimental import pallas as pl
from jax.experimental.pallas import tpu as pltpu

F32 = jnp.float32
BF16 = jnp.bfloat16

D_MODEL = 1024
DEPTH = 4
N_CTX_SEQ, CTX_LEN = 16, 256
N_LAT_SEQ, LAT_LEN = 2, 2048
T_CTX = N_CTX_SEQ * CTX_LEN
T_LAT = N_LAT_SEQ * LAT_LEN
T_ALL = T_CTX + T_LAT
GRID_W = 64
GRID_H = LAT_LEN // GRID_W
CHUNK = 128
D_GMLP = 512
D_MLSTM = 512
N_HEADS = 4
DH = 128
N_KEYS = 128
N_EXPERTS = N_KEYS * N_KEYS
PEER_HEADS = 8
PEER_TOPK = 16
D_PROJ = 2 * D_GMLP + 4 * D_MLSTM
N_GATES = 16
EPS = 1e-6
NEG_INF = float("-inf")
POS_INF = float("inf")

VMEM_LIMIT_BYTES = 56 * 1024 * 1024

COL_U, COL_VS, COL_QK, COL_V, COL_O = 0, 512, 1024, 2048, 2560


def _dot(a, b):
    return jnp.dot(a, b, preferred_element_type=F32)


def _dot_nt(a, b):
    return lax.dot_general(a, b, (((1,), (1,)), ((), ())), preferred_element_type=F32)


def _dot_tn(a, b):
    return lax.dot_general(a, b, (((0,), (0,)), ((), ())), preferred_element_type=F32)


def _split_bf16(x):
    hi = x.astype(BF16)
    lo = (x - hi.astype(F32)).astype(BF16)
    return hi, lo


def _rms(x):
    return x * lax.rsqrt(jnp.mean(x * x, axis=-1, keepdims=True) + EPS)


def _sigmoid(x):
    return 1.0 / (1.0 + jnp.exp(-x))


def _gelu_tanh(x):
    return 0.5 * x * (1.0 + jnp.tanh(0.7978845608028654 * (x + 0.044715 * (x * x * x))))


def _segment(i, tile):
    start = i * tile
    return (start >= T_CTX).astype(jnp.int32) + (start >= T_CTX + LAT_LEN).astype(jnp.int32)


def _params(*sem):
    return pltpu.CompilerParams(dimension_semantics=sem, vmem_limit_bytes=VMEM_LIMIT_BYTES)


MOD_TN = 1536


def _mod_kernel(c_ref, w_ref, b_ref, o_ref):
    c = c_ref[...]
    a = (c * _sigmoid(c)).astype(BF16)
    o_ref[...] = _dot(a, w_ref[...].astype(BF16)) + b_ref[...]


def _modulation(cond, w_mod, b_mod):
    n = w_mod.shape[-1]
    return pl.pallas_call(
        _mod_kernel,
        out_shape=jax.ShapeDtypeStruct((DEPTH, 8, n), F32),
        grid=(DEPTH, n // MOD_TN),
        in_specs=[
            pl.BlockSpec((8, D_MODEL), lambda l, j: (0, 0)),
            pl.BlockSpec((None, D_MODEL, MOD_TN), lambda l, j: (l, 0, j)),
            pl.BlockSpec((None, 1, MOD_TN), lambda l, j: (l, 0, j)),
        ],
        out_specs=pl.BlockSpec((None, 8, MOD_TN), lambda l, j: (l, 0, j)),
        compiler_params=_params("arbitrary", "arbitrary"),
        name="modulation",
    )(cond, w_mod, b_mod.reshape(DEPTH, 1, n))


IN_TM = 512
IN_TN = 512
GATE_PAD = 128


def _inproj_kernel(x_ref, mod_ref, w_ref, wg_ref, wgt_ref, bg_ref, bgt_ref, y_ref, g_ref, gt_ref):
    x = x_ref[...]
    h = _rms(x) * (1.0 + mod_ref[1:2, :]) + mod_ref[0:1, :]
    hb = h.astype(BF16)
    for n0 in range(0, D_PROJ, IN_TN):
        y_ref[:, n0:n0 + IN_TN] = _dot(hb, w_ref[:, n0:n0 + IN_TN])

    def gate_act(g, idx):
        log_sig = jnp.minimum(g, 0.0) - jnp.log(1.0 + jnp.exp(-jnp.abs(g)))
        return jnp.where((idx & 4) != 0, log_sig, g)

    g = _dot(hb, wg_ref[...]) + bg_ref[...]
    g_ref[...] = gate_act(g, lax.broadcasted_iota(jnp.int32, g.shape, 1))
    gt = _dot_nt(wgt_ref[...], hb) + bgt_ref[...]
    gt_ref[...] = gate_act(gt, lax.broadcasted_iota(jnp.int32, gt.shape, 0))


def _inproj(x, mod_l, w_main, wg, wgt, bg, bgt):
    return pl.pallas_call(
        _inproj_kernel,
        out_shape=(
            jax.ShapeDtypeStruct((T_ALL, D_PROJ), F32),
            jax.ShapeDtypeStruct((T_ALL, GATE_PAD), F32),
            jax.ShapeDtypeStruct((N_GATES, T_ALL), F32),
        ),
        grid=(T_ALL // IN_TM,),
        in_specs=[
            pl.BlockSpec((IN_TM, D_MODEL), lambda i: (i, 0)),
            pl.BlockSpec((None, 6, D_MODEL), lambda i: (_segment(i, IN_TM), 0, 0)),
            pl.BlockSpec((D_MODEL, D_PROJ), lambda i: (0, 0)),
            pl.BlockSpec((D_MODEL, GATE_PAD), lambda i: (0, 0)),
            pl.BlockSpec((N_GATES, D_MODEL), lambda i: (0, 0)),
            pl.BlockSpec((1, GATE_PAD), lambda i: (0, 0)),
            pl.BlockSpec((N_GATES, 1), lambda i: (0, 0)),
        ],
        out_specs=(
            pl.BlockSpec((IN_TM, D_PROJ), lambda i: (i, 0)),
            pl.BlockSpec((IN_TM, GATE_PAD), lambda i: (i, 0)),
            pl.BlockSpec((N_GATES, IN_TM), lambda i: (0, i)),
        ),
        compiler_params=_params("arbitrary"),
        name="inproj",
    )(x, mod_l, w_main, wg, wgt, bg, bgt)


CONV_CB = 128
CONV_ROWS = 4096


def _conv_taps(x, w_ref, taps):
    rows = x.shape[0]
    acc = None
    for w_row, delta, valid in taps:
        shifted = x if delta == 0 else pltpu.roll(x, (-delta) % rows, 0)
        term = shifted * w_ref[w_row:w_row + 1, :]
        if valid is not None:
            term = jnp.where(valid, term, 0.0)
        acc = term if acc is None else acc + term
    return acc


def _conv_kernel(x_ref, w_ref, o_ref):
    grp = pl.program_id(0)
    cblk = pl.program_id(1)
    scale = jnp.where(cblk >= (D_MLSTM // CONV_CB), DH ** -0.5, 1.0).astype(F32)

    def finish(y):
        o_ref[...] = (y * _sigmoid(y)) * scale

    @pl.when(grp == 0)
    def _ctx():
        x = x_ref[...]
        pos = lax.broadcasted_iota(jnp.int32, x.shape, 0) & (CTX_LEN - 1)
        taps = [(3, -1, pos >= 1), (4, 0, None), (5, 1, pos <= CTX_LEN - 2)]
        finish(_conv_taps(x, w_ref, taps))

    @pl.when(grp == 1)
    def _lat():
        x = x_ref[...]
        pos = lax.broadcasted_iota(jnp.int32, x.shape, 0) & (LAT_LEN - 1)
        r = pos >> 6
        c = pos & (GRID_W - 1)
        row_ok = {-1: r >= 1, 0: None, 1: r <= GRID_H - 2}
        col_ok = {-1: c >= 1, 0: None, 1: c <= GRID_W - 2}
        taps = []
        for di in (-1, 0, 1):
            for dj in (-1, 0, 1):
                ok = row_ok[di]
                if col_ok[dj] is not None:
                    ok = col_ok[dj] if ok is None else (ok & col_ok[dj])
                taps.append(((di + 1) * 3 + (dj + 1), di * GRID_W + dj, ok))
        finish(_conv_taps(x, w_ref, taps))


def _conv(y, conv_w9):
    qk_blk0 = COL_QK // CONV_CB
    return pl.pallas_call(
        _conv_kernel,
        out_shape=jax.ShapeDtypeStruct((T_ALL, 2 * D_MLSTM), F32),
        grid=(T_ALL // CONV_ROWS, 2 * D_MLSTM // CONV_CB),
        in_specs=[
            pl.BlockSpec((CONV_ROWS, CONV_CB), lambda g, j: (g, qk_blk0 + j)),
            pl.BlockSpec((9, CONV_CB), lambda g, j: (0, j)),
        ],
        out_specs=pl.BlockSpec((CONV_ROWS, CONV_CB), lambda g, j: (g, j)),
        compiler_params=_params("arbitrary", "arbitrary"),
        name="conv_silu",
    )(y, conv_w9)


def _mlstm_direction(d, q_ref, k_ref, v_ref, g_ref, gt_ref, h_ref, cext_ref, m_ref):
    row = lax.broadcasted_iota(jnp.int32, (CHUNK, CHUNK), 0)
    col = lax.broadcasted_iota(jnp.int32, (CHUNK, CHUNK), 1)
    mask = (col <= row) if d == 0 else (col >= row)
    tri = jnp.where(mask, 1.0, 0.0).astype(BF16)
    g = g_ref[...]
    gt = gt_ref[...]
    g_hi, g_lo = _split_bf16(g)
    gt_hi, gt_lo = _split_bf16(gt)
    cum_col = _dot(tri, g_hi) + _dot(tri, g_lo)
    cum_row = _dot_nt(gt_hi, tri) + _dot_nt(gt_lo, tri)
    ones = jnp.ones((CHUNK, DH), BF16)
    for h in range(N_HEADS):
        ci = d * 8 + h
        cf = d * 8 + 4 + h
        sl = slice(h * DH, (h + 1) * DH)
        q = q_ref[:, sl].astype(BF16)
        k = k_ref[:, sl]
        v_ext = jnp.concatenate([v_ref[:, sl].astype(BF16), ones], axis=1)
        b_col = cum_col[:, cf:cf + 1]
        i_col = g[:, ci:ci + 1]
        b_row = cum_row[cf:cf + 1, :]
        i_row = gt[ci:ci + 1, :]
        m_prev = m_ref[d * 4 + h:d * 4 + h + 1, 0:1]
        dmat = jnp.where(mask, b_col - b_row + i_row, NEG_INF)
        prior = b_col + m_prev
        m_t = jnp.maximum(prior, jnp.max(dmat, axis=1, keepdims=True))
        w_intra = jnp.exp(dmat - m_t)
        w_prior = jnp.exp(prior - m_t)
        s = _dot_nt(q, k.astype(BF16)) * w_intra
        cext = cext_ref[d, h]
        tot = w_prior * _dot(q, cext.astype(BF16)) + _dot(s.astype(BF16), v_ext)
        num = tot[:, :DH]
        den = tot[:, DH:]
        h_ref[:, sl] = num / jnp.maximum(jnp.abs(den), jnp.exp(-m_t))
        b_last = b_col[CHUNK - 1:CHUNK, :] if d == 0 else b_col[0:1, :]
        log_end = b_last - b_col + i_col
        m_new = jnp.maximum(b_last + m_prev, jnp.max(log_end, axis=0, keepdims=True))
        w_end = jnp.exp(log_end - m_new)
        decay = jnp.exp(b_last + m_prev - m_new)
        cext_ref[d, h] = decay * cext + _dot_tn((k * w_end).astype(BF16), v_ext)
        m_ref[d * 4 + h:d * 4 + h + 1, :] = jnp.broadcast_to(m_new, (1, DH))


def _mlstm_kernel(qf_ref, kf_ref, vf_ref, gf_ref, gtf_ref, qb_ref, kb_ref, vb_ref, gb_ref, gtb_ref,
                  c0_ref, n0_ref, m0_ref, hf_ref, hb_ref, cn_ref, nn_ref, mn_ref, cext_ref, m_ref):
    c = pl.program_id(1)

    @pl.when(c == 0)
    def _init():
        for d in range(2):
            for h in range(N_HEADS):
                cext_ref[d, h, :, 0:DH] = c0_ref[d, h]
                cext_ref[d, h, :, DH:2 * DH] = jnp.broadcast_to(n0_ref[d, h], (DH, DH))
        m_ref[...] = m0_ref[...]

    _mlstm_direction(0, qf_ref, kf_ref, vf_ref, gf_ref, gtf_ref, hf_ref, cext_ref, m_ref)
    _mlstm_direction(1, qb_ref, kb_ref, vb_ref, gb_ref, gtb_ref, hb_ref, cext_ref, m_ref)

    @pl.when(c == pl.num_programs(1) - 1)
    def _final():
        for d in range(2):
            for h in range(N_HEADS):
                cn_ref[d, h] = cext_ref[d, h, :, 0:DH]
                nn_ref[d, h] = cext_ref[d, h, :, DH:DH + 1]
        mn_ref[...] = m_ref[...]


def _mlstm(qk, y, g, gt, c0, n0, m0, n_seq, seq_len, tok0):
    nc = seq_len // CHUNK
    blk0 = tok0 // CHUNK
    vcol = COL_V // D_MLSTM

    def chunk_specs(chunk_of):
        tok = lambda b, c: blk0 + b * nc + chunk_of(c)
        return [
            pl.BlockSpec((CHUNK, D_MLSTM), lambda b, c: (tok(b, c), 0)),
            pl.BlockSpec((CHUNK, D_MLSTM), lambda b, c: (tok(b, c), 1)),
            pl.BlockSpec((CHUNK, D_MLSTM), lambda b, c: (tok(b, c), vcol)),
            pl.BlockSpec((CHUNK, GATE_PAD), lambda b, c: (tok(b, c), 0)),
            pl.BlockSpec((N_GATES, CHUNK), lambda b, c: (0, tok(b, c))),
        ]

    fwd_chunk = lambda c: c
    bwd_chunk = lambda c: nc - 1 - c
    state_specs = [
        pl.BlockSpec((None, 2, N_HEADS, DH, DH), lambda b, c: (b, 0, 0, 0, 0)),
        pl.BlockSpec((None, 2, N_HEADS, DH, 1), lambda b, c: (b, 0, 0, 0, 0)),
        pl.BlockSpec((None, 2 * N_HEADS, DH), lambda b, c: (b, 0, 0)),
    ]
    n_tok = n_seq * seq_len
    return pl.pallas_call(
        _mlstm_kernel,
        out_shape=(
            jax.ShapeDtypeStruct((n_tok, D_MLSTM), F32),
            jax.ShapeDtypeStruct((n_tok, D_MLSTM), F32),
            jax.ShapeDtypeStruct((n_seq, 2, N_HEADS, DH, DH), F32),
            jax.ShapeDtypeStruct((n_seq, 2, N_HEADS, DH, 1), F32),
            jax.ShapeDtypeStruct((n_seq, 2 * N_HEADS, DH), F32),
        ),
        grid=(n_seq, nc),
        in_specs=chunk_specs(fwd_chunk) + chunk_specs(bwd_chunk) + state_specs,
        out_specs=(
            pl.BlockSpec((CHUNK, D_MLSTM), lambda b, c: (b * nc + fwd_chunk(c), 0)),
            pl.BlockSpec((CHUNK, D_MLSTM), lambda b, c: (b * nc + bwd_chunk(c), 0)),
            state_specs[0], state_specs[1], state_specs[2],
        ),
        scratch_shapes=[pltpu.VMEM((2, N_HEADS, DH, 2 * DH), F32), pltpu.VMEM((2 * N_HEADS, DH), F32)],
        compiler_params=_params("arbitrary", "arbitrary"),
        name="mlstm_scan",
    )(qk, qk, y, g, gt, qk, qk, y, g, gt, c0, n0, m0)


MIX_TM = 512


def _mix_kernel(x_ref, u_ref, vs_ref, o_ref, hf_ref, hb_ref, mod_ref, ws_ref, bs_ref, nw_ref,
                wout_ref, xo_ref, h2_ref, h2t_ref, cat_ref):
    for c0 in range(0, MIX_TM, CHUNK):
        rows = slice(c0, c0 + CHUNK)
        for h in range(N_HEADS):
            sl = slice(h * DH, (h + 1) * DH)
            vh = _rms(vs_ref[rows, sl]).astype(BF16)
            mixed = _dot(ws_ref[h], vh) + bs_ref[h]
            cat_ref[rows, sl] = (u_ref[rows, sl] * mixed).astype(BF16)
            hm = _rms(hf_ref[rows, sl] + hb_ref[rows, sl]) * nw_ref[:, sl]
            cat_ref[rows, D_GMLP + h * DH:D_GMLP + (h + 1) * DH] = (
                hm * _sigmoid(o_ref[rows, sl])).astype(BF16)
    mix = _dot(cat_ref[...], wout_ref[...])
    x_new = x_ref[...] + mod_ref[2:3, :] * mix
    xo_ref[...] = x_new
    h2 = (_rms(x_new) * (1.0 + mod_ref[4:5, :]) + mod_ref[3:4, :]).astype(BF16)
    h2_ref[...] = h2
    h2t_ref[...] = h2.T


def _mix(x, y, hf, hb, mod_l, ws, bs_b, nw, wout):
    blk = lambda col: (lambda i: (i, col))
    return pl.pallas_call(
        _mix_kernel,
        out_shape=(
            jax.ShapeDtypeStruct((T_ALL, D_MODEL), F32),
            jax.ShapeDtypeStruct((T_ALL, D_MODEL), BF16),
            jax.ShapeDtypeStruct((D_MODEL, T_ALL), BF16),
        ),
        grid=(T_ALL // MIX_TM,),
        in_specs=[
            pl.BlockSpec((MIX_TM, D_MODEL), blk(0)),
            pl.BlockSpec((MIX_TM, D_GMLP), blk(COL_U // D_GMLP)),
            pl.BlockSpec((MIX_TM, D_GMLP), blk(COL_VS // D_GMLP)),
            pl.BlockSpec((MIX_TM, D_MLSTM), blk(COL_O // D_MLSTM)),
            pl.BlockSpec((MIX_TM, D_MLSTM), blk(0)),
            pl.BlockSpec((MIX_TM, D_MLSTM), blk(0)),
            pl.BlockSpec((None, 6, D_MODEL), lambda i: (_segment(i, MIX_TM), 0, 0)),
            pl.BlockSpec((N_HEADS, CHUNK, CHUNK), lambda i: (0, 0, 0)),
            pl.BlockSpec((N_HEADS, CHUNK, DH), lambda i: (0, 0, 0)),
            pl.BlockSpec((1, D_MLSTM), lambda i: (0, 0)),
            pl.BlockSpec((D_MODEL, D_MODEL), lambda i: (0, 0)),
        ],
        out_specs=(
            pl.BlockSpec((MIX_TM, D_MODEL), blk(0)),
            pl.BlockSpec((MIX_TM, D_MODEL), blk(0)),
            pl.BlockSpec((D_MODEL, MIX_TM), lambda i: (0, i)),
        ),
        scratch_shapes=[pltpu.VMEM((MIX_TM, D_MODEL), BF16)],
        compiler_params=_params("arbitrary"),
        name="mixer_out",
    )(x, y, y, y, hf, hb, mod_l, ws, bs_b, nw, wout)


SEL_TM = 1024
_PAIR_COLS = [PEER_TOPK // (i + 1) for i in range(PEER_TOPK)]


LANES = 128
SUBLANES = 8


def _batcher_pairs(n):
    pairs = []
    p = 1
    while p < n:
        k = p
        while k >= 1:
            for j in range(k % p, n - k, 2 * k):
                for i in range(min(k, n - j - k)):
                    if (i + j) // (2 * p) == (i + j + k) // (2 * p):
                        pairs.append((i + j, i + j + k))
            k //= 2
        p *= 2
    return pairs


_SORT16 = tuple(_batcher_pairs(PEER_TOPK))


def _compare_exchange(v, a, b):
    x, y = v[a], v[b]
    if y is None:
        return
    if x is None:
        v[a], v[b] = y, None
        return
    v[a], v[b] = jnp.maximum(x, y), jnp.minimum(x, y)


def _top16_sorted(tiles):
    v = list(tiles) + [None] * (PEER_TOPK - len(tiles))
    for a, b in _SORT16:
        _compare_exchange(v, a, b)
    for shift in (4, 2, 1):
        other = [None if x is None else pltpu.roll(x, shift, 0) for x in v]
        merged = []
        for i in range(PEER_TOPK):
            x, y = v[i], other[PEER_TOPK - 1 - i]
            merged.append(y if x is None else x if y is None else jnp.maximum(x, y))
        v = merged
        dist = PEER_TOPK // 2
        while dist >= 1:
            for i in range(PEER_TOPK):
                if (i & dist) == 0:
                    _compare_exchange(v, i, i + dist)
            dist //= 2
    return v


def _rows_to_tile(rows, sub):
    out = rows[0]
    for r in range(1, SUBLANES):
        out = jnp.where(sub == r, rows[r], out)
    return out


def _select_chunk(s1_tiles, s2_tiles):
    v1 = _top16_sorted(s1_tiles)
    v2 = _top16_sorted(s2_tiles)
    sub = lax.broadcasted_iota(jnp.int32, (SUBLANES, LANES), 0)
    v2_lo = _rows_to_tile(v2[0:8], sub)
    v2_hi = _rows_to_tile(v2[8:16], sub)
    v1_hi = _rows_to_tile(v1[8:16], sub)
    cands = [v1[0] + v2_lo, v1[0] + v2_hi]
    for i in range(1, 8):
        cands.append(jnp.where(sub < _PAIR_COLS[i], v1[i] + v2_lo, NEG_INF))
    cands.append(v1_hi + v2[0])
    tau = _top16_sorted(cands)[PEER_TOPK - 1]
    best = v1[0] + v2[0]
    z_tile = None
    count_rank = []
    for idx, c in enumerate(cands[:9]):
        sel = c >= tau
        e = jnp.where(sel, jnp.exp(c - best), 0.0)
        z_tile = e if z_tile is None else z_tile + e
        n = jnp.where(sel, 1.0, 0.0)
        if idx == 1:
            count_rank[0] = count_rank[0] + n
        else:
            count_rank.append(n)
    z_tile = z_tile + jnp.where(cands[9] >= tau, jnp.exp(cands[9] - best), 0.0)
    inv_z = 1.0 / jnp.broadcast_to(jnp.sum(z_tile, axis=0, keepdims=True), (SUBLANES, LANES))
    count_rank = [jnp.broadcast_to(jnp.sum(n, axis=0, keepdims=True), (SUBLANES, LANES)) for n in count_rank]
    rank2, p2, count1, p1 = [], [], [], []
    for s1, s2 in zip(s1_tiles, s2_tiles):
        n = jnp.where(s1 + v2[0] >= tau, 1.0, 0.0)
        for i in range(8):
            n = jnp.where(s1 == v1[i], count_rank[i], n)
        count1.append(n)
        p1.append(jnp.exp(s1 - v1[0]) * inv_z)
        r = jnp.full((SUBLANES, LANES), float(PEER_TOPK), F32)
        for jr in reversed(range(PEER_TOPK)):
            r = jnp.where(s2 == v2[jr], float(jr), r)
        rank2.append(r)
        p2.append(jnp.exp(s2 - v2[0]))
    return rank2, p2, count1, p1


def _peer_sel_kernel(h_ref, wq_ref, keys_ref, r2_ref, p2_ref, n1_ref, p1_ref, sc_ref):
    q = _dot(h_ref[...], wq_ref[...])
    for p in range(2):
        sc_ref[p] = _dot_nt(keys_ref[p], q[:, p * 128:(p + 1) * 128].astype(BF16))

    def chunk(j, carry):
        lanes = pl.ds(pl.multiple_of(j * LANES, LANES), LANES)
        tile = lambda p, k: sc_ref[p, k * SUBLANES:(k + 1) * SUBLANES, lanes]
        s1 = [tile(0, k) for k in range(N_KEYS // SUBLANES)]
        s2 = [tile(1, k) for k in range(N_KEYS // SUBLANES)]
        rank2, p2, count1, p1 = _select_chunk(s1, s2)
        for k in range(N_KEYS // SUBLANES):
            rows = slice(k * SUBLANES, (k + 1) * SUBLANES)
            n1_ref[j, rows, :] = count1[k]
            p1_ref[j, rows, :] = p1[k]
        for k in range(0, N_KEYS // SUBLANES, 2):
            rows = slice((k // 2) * SUBLANES, (k // 2 + 1) * SUBLANES)
            r2_ref[j, rows, :] = pltpu.bitcast(jnp.concatenate(rank2[k:k + 2], axis=0).astype(BF16), jnp.uint32)
            p2_ref[j, rows, :] = pltpu.bitcast(jnp.concatenate(p2[k:k + 2], axis=0).astype(BF16), jnp.uint32)
        return carry

    lax.fori_loop(0, SEL_TM // LANES, chunk, 0)


def _peer_select(h2, wq, keys):
    tab = jax.ShapeDtypeStruct((PEER_HEADS, T_ALL // LANES, N_KEYS, LANES), F32)
    tab16 = jax.ShapeDtypeStruct((PEER_HEADS, T_ALL // LANES, N_KEYS // 2, LANES), jnp.uint32)
    tab_spec = pl.BlockSpec((None, SEL_TM // LANES, N_KEYS, LANES), lambda i, h: (h, i, 0, 0))
    tab16_spec = pl.BlockSpec((None, SEL_TM // LANES, N_KEYS // 2, LANES), lambda i, h: (h, i, 0, 0))
    return pl.pallas_call(
        _peer_sel_kernel,
        out_shape=(tab16, tab16, tab, tab),
        grid=(T_ALL // SEL_TM, PEER_HEADS),
        in_specs=[
            pl.BlockSpec((SEL_TM, D_MODEL), lambda i, h: (i, 0)),
            pl.BlockSpec((D_MODEL, 2 * 128), lambda i, h: (0, h)),
            pl.BlockSpec((None, 2, N_KEYS, 128), lambda i, h: (h, 0, 0, 0)),
        ],
        out_specs=(tab16_spec, tab16_spec, tab_spec, tab_spec),
        scratch_shapes=[pltpu.VMEM((2, N_KEYS, SEL_TM), F32)],
        compiler_params=_params("arbitrary", "arbitrary"),
        name="peer_select",
    )(h2, wq, keys)


EXP_TM = 512
EXP_KEY_ROWS = 8
EXP_TE = EXP_KEY_ROWS * N_KEYS
EXP_STEPS = N_EXPERTS // EXP_TE
EXP_CHUNK = 2 * N_KEYS
EXP_CHUNKS = EXP_TE // EXP_CHUNK


def _peer_expert_kernel(ht_ref, u_ref, unext_ref, vt_ref, r2_ref, p2_ref, n1_ref, p1_ref,
                        x_ref, mod_ref, o_ref, act_ref, gate_ref, acc_ref):
    j = pl.program_id(1)

    @pl.when(j == 0)
    def _head():
        acc_ref[...] = jnp.zeros_like(acc_ref)
        act_ref[0] = _dot(u_ref[0:EXP_CHUNK, :], ht_ref[...])

    part = None
    for c in range(EXP_CHUNKS):
        anchor = None
        for lc in range(EXP_TM // LANES):
            lanes = slice(lc * LANES, (lc + 1) * LANES)
            w = [None, None]
            for h in range(PEER_HEADS):
                rank2 = pltpu.bitcast(r2_ref[h, lc], BF16).reshape(N_KEYS // 16, 16, LANES)
                p2 = pltpu.bitcast(p2_ref[h, lc], BF16).reshape(N_KEYS // 16, 16, LANES)
                for k in range(2):
                    r = 2 * c + k
                    count1 = jnp.broadcast_to(n1_ref[h, lc, r:r + 1, :], (16, LANES)).astype(BF16)
                    p1 = jnp.broadcast_to(p1_ref[h, lc, r:r + 1, :], (16, LANES)).astype(BF16)
                    term = jnp.where(rank2 < count1[None], p2, jnp.zeros_like(p2)) * p1[None]
                    w[k] = term if w[k] is None else w[k] + term
            for k in range(2):
                rows = slice(k * N_KEYS, (k + 1) * N_KEYS)
                act = act_ref[c, rows, lanes].reshape(N_KEYS // 16, 16, LANES)
                g = (w[k] * _gelu_tanh(act).astype(BF16)).reshape(N_KEYS, LANES)
                gate_ref[c, rows, lanes] = g
                if anchor is None:
                    anchor = g[0:16, :]
        zero = pltpu.bitcast((pltpu.bitcast(anchor, jnp.uint32) >> 16) >> 16, BF16)
        nxt = unext_ref[...] if c == EXP_CHUNKS - 1 else u_ref[(c + 1) * EXP_CHUNK:(c + 2) * EXP_CHUNK, :]
        nxt = nxt + jnp.tile(zero, (EXP_CHUNK // 16, D_MODEL // LANES))
        act_ref[(c + 1) % EXP_CHUNKS] = _dot(nxt, ht_ref[...])
        d = _dot(vt_ref[:, c * EXP_CHUNK:(c + 1) * EXP_CHUNK], gate_ref[c])
        part = d if part is None else part + d
    acc_ref[...] += part

    @pl.when(j == EXP_STEPS - 1)
    def _tail():
        o_ref[...] = x_ref[...] + mod_ref[5:6, :] * acc_ref[...].T


def _peer_experts(h2t, u, vt, r2, p2, n1, p1, x, mod_l):
    n_chunks = N_EXPERTS // EXP_CHUNK
    tab_spec = pl.BlockSpec((PEER_HEADS, EXP_TM // LANES, N_KEYS // 2, LANES), lambda i, j: (0, i, 0, 0))
    row_spec = pl.BlockSpec((PEER_HEADS, EXP_TM // LANES, EXP_KEY_ROWS, LANES), lambda i, j: (0, i, j, 0))
    return pl.pallas_call(
        _peer_expert_kernel,
        out_shape=jax.ShapeDtypeStruct((T_ALL, D_MODEL), F32),
        grid=(T_ALL // EXP_TM, EXP_STEPS),
        in_specs=[
            pl.BlockSpec((D_MODEL, EXP_TM), lambda i, j: (0, i)),
            pl.BlockSpec((EXP_TE, D_MODEL), lambda i, j: (j, 0)),
            pl.BlockSpec((EXP_CHUNK, D_MODEL),
                         lambda i, j: (jnp.minimum((j + 1) * EXP_CHUNKS, n_chunks - 1), 0)),
            pl.BlockSpec((D_MODEL, EXP_TE), lambda i, j: (0, j)),
            tab_spec, tab_spec, row_spec, row_spec,
            pl.BlockSpec((EXP_TM, D_MODEL), lambda i, j: (i, 0)),
            pl.BlockSpec((None, 6, D_MODEL), lambda i, j: (_segment(i, EXP_TM), 0, 0)),
        ],
        out_specs=pl.BlockSpec((EXP_TM, D_MODEL), lambda i, j: (i, 0)),
        scratch_shapes=[
            pltpu.VMEM((EXP_CHUNKS, EXP_CHUNK, EXP_TM), F32),
            pltpu.VMEM((EXP_CHUNKS, EXP_CHUNK, EXP_TM), BF16),
            pltpu.VMEM((D_MODEL, EXP_TM), F32),
        ],
        compiler_params=_params("arbitrary", "arbitrary"),
        name="peer_experts",
    )(h2t, u, u, vt, r2, p2, n1, p1, x, mod_l)


FIN_TM = 1024


def _final_kernel(x_ref, w_ref, o_ref):
    o_ref[...] = _rms(x_ref[...]) * w_ref[...]


def _final_norm(x, w):
    return pl.pallas_call(
        _final_kernel,
        out_shape=jax.ShapeDtypeStruct((T_ALL, D_MODEL), F32),
        grid=(T_ALL // FIN_TM,),
        in_specs=[pl.BlockSpec((FIN_TM, D_MODEL), lambda i: (i, 0)),
                  pl.BlockSpec((1, D_MODEL), lambda i: (0, 0))],
        out_specs=pl.BlockSpec((FIN_TM, D_MODEL), lambda i: (i, 0)),
        compiler_params=_params("arbitrary"),
        name="final_norm",
    )(x, w)


def kernel(x_prompt, x_sample, state_C, state_n, state_m, c, c_ctx, w_mod, b_mod, w_in, b_gate,
           conv_w, w_s, b_s, mlstm_norm_w, w_out, peer_wq, peer_keys, peer_u, peer_v, final_norm_w):
    assert x_prompt.shape == (N_CTX_SEQ, CTX_LEN, D_MODEL)
    assert x_sample.shape == (N_LAT_SEQ, LAT_LEN, D_MODEL)
    x = jnp.concatenate([x_prompt.reshape(T_CTX, D_MODEL), x_sample.reshape(T_LAT, D_MODEL)], axis=0)

    cond = jnp.zeros((8, D_MODEL), F32).at[0].set(c_ctx).at[1:1 + N_LAT_SEQ].set(c)
    mod = _modulation(cond, w_mod, b_mod)[:, :1 + N_LAT_SEQ].reshape(DEPTH, 1 + N_LAT_SEQ, 6, D_MODEL)

    zero_c = jnp.zeros((N_CTX_SEQ, 2, N_HEADS, DH, DH), F32)
    zero_n = jnp.zeros((N_CTX_SEQ, 2, N_HEADS, DH, 1), F32)
    zero_m = jnp.zeros((N_CTX_SEQ, 2 * N_HEADS, DH), F32)

    new_c, new_n, new_m = [], [], []
    for l in range(DEPTH):
        w_main = w_in[l, :, :D_PROJ].astype(BF16)
        w_gate = w_in[l, :, D_PROJ:]
        wg = jnp.pad(w_gate, ((0, 0), (0, GATE_PAD - N_GATES))).astype(BF16)
        wgt = w_gate.T.astype(BF16)
        bg = jnp.pad(b_gate[l].reshape(1, N_GATES), ((0, 0), (0, GATE_PAD - N_GATES)))
        bgt = b_gate[l].reshape(N_GATES, 1)
        y, g, gt = _inproj(x, mod[l], w_main, wg, wgt, bg, bgt)

        qk = _conv(y, conv_w[l].reshape(9, 2 * D_MLSTM))

        hf_c, hb_c, c_ctx_new, n_ctx_new, m_ctx_new = _mlstm(
            qk, y, g, gt, zero_c, zero_n, zero_m, N_CTX_SEQ, CTX_LEN, 0)
        lat_m0 = jnp.broadcast_to(state_m[:, l].reshape(N_LAT_SEQ, 2 * N_HEADS, 1),
                                  (N_LAT_SEQ, 2 * N_HEADS, DH))
        hf_l, hb_l, _, _, _ = _mlstm(
            qk, y, g, gt, state_C[:, l], state_n[:, l][..., None], lat_m0, N_LAT_SEQ, LAT_LEN, T_CTX)
        hf = jnp.concatenate([hf_c, hf_l], axis=0)
        hb = jnp.concatenate([hb_c, hb_l], axis=0)
        new_c.append(c_ctx_new)
        new_n.append(n_ctx_new[..., 0])
        new_m.append(m_ctx_new[..., 0].reshape(N_CTX_SEQ, 2, N_HEADS))

        bs_b = jnp.broadcast_to(b_s[l][:, :, None], (N_HEADS, CHUNK, DH))
        x, h2, h2t = _mix(x, y, hf, hb, mod[l], w_s[l].astype(BF16), bs_b,
                          mlstm_norm_w[l].reshape(1, D_MLSTM), w_out[l].astype(BF16))

        rank2, p2, count1, p1 = _peer_select(h2, peer_wq[l].astype(BF16), peer_keys[l].astype(BF16))
        x = _peer_experts(h2t, peer_u[l].astype(BF16), peer_v[l].T.astype(BF16),
                          rank2, p2, count1, p1, x, mod[l])

    y_all = _final_norm(x, final_norm_w.reshape(1, D_MODEL))
    y_prompt = y_all[:T_CTX].reshape(N_CTX_SEQ, CTX_LEN, D_MODEL)
    y_sample = y_all[T_CTX:].reshape(N_LAT_SEQ, LAT_LEN, D_MODEL)
    return (y_prompt, y_sample, jnp.stack(new_c, axis=1), jnp.stack(new_n, axis=1),
            jnp.stack(new_m, axis=1))
```

```python
import jax
import jax.numpy as jnp
from jax import lax
from jax.experimental import pallas as pl
from jax.experimental.pallas import tpu as pltpu

F32 = jnp.float32
BF16 = jnp.bfloat16

D_MODEL = 1024
DEPTH = 4
N_CTX_SEQ, CTX_LEN = 16, 256
N_LAT_SEQ, LAT_LEN = 2, 2048
T_CTX = N_CTX_SEQ * CTX_LEN
T_LAT = N_LAT_SEQ * LAT_LEN
T_ALL = T_CTX + T_LAT
GRID_W = 64
GRID_H = LAT_LEN // GRID_W
CHUNK = 128
D_GMLP = 512
D_MLSTM = 512
N_HEADS = 4
DH = 128
N_KEYS = 128
N_EXPERTS = N_KEYS * N_KEYS
PEER_HEADS = 8
PEER_TOPK = 16
D_PROJ = 2 * D_GMLP + 4 * D_MLSTM
N_GATES = 16
EPS = 1e-6
NEG_INF = float("-inf")
POS_INF = float("inf")

VMEM_LIMIT_BYTES = 56 * 1024 * 1024

COL_U, COL_VS, COL_QK, COL_V, COL_O = 0, 512, 1024, 2048, 2560


def _dot(a, b):
    return jnp.dot(a, b, preferred_element_type=F32)


def _dot_nt(a, b):
    return lax.dot_general(a, b, (((1,), (1,)), ((), ())), preferred_element_type=F32)


def _dot_tn(a, b):
    return lax.dot_general(a, b, (((0,), (0,)), ((), ())), preferred_element_type=F32)


def _split_bf16(x):
    hi = x.astype(BF16)
    lo = (x - hi.astype(F32)).astype(BF16)
    return hi, lo


def _rms(x):
    return x * lax.rsqrt(jnp.mean(x * x, axis=-1, keepdims=True) + EPS)


def _sigmoid(x):
    return 1.0 / (1.0 + jnp.exp(-x))


def _gelu_tanh(x):
    return 0.5 * x * (1.0 + jnp.tanh(0.7978845608028654 * (x + 0.044715 * (x * x * x))))


def _segment(i, tile):
    start = i * tile
    return (start >= T_CTX).astype(jnp.int32) + (start >= T_CTX + LAT_LEN).astype(jnp.int32)


def _params(*sem):
    return pltpu.CompilerParams(dimension_semantics=sem, vmem_limit_bytes=VMEM_LIMIT_BYTES)


MOD_TN = 1536


def _mod_kernel(c_ref, w_ref, b_ref, o_ref):
    c = c_ref[...]
    a = (c * _sigmoid(c)).astype(BF16)
    o_ref[...] = _dot(a, w_ref[...].astype(BF16)) + b_ref[...]


def _modulation(cond, w_mod, b_mod):
    n = w_mod.shape[-1]
    return pl.pallas_call(
        _mod_kernel,
        out_shape=jax.ShapeDtypeStruct((DEPTH, 8, n), F32),
        grid=(DEPTH, n // MOD_TN),
        in_specs=[
            pl.BlockSpec((8, D_MODEL), lambda l, j: (0, 0)),
            pl.BlockSpec((None, D_MODEL, MOD_TN), lambda l, j: (l, 0, j)),
            pl.BlockSpec((None, 1, MOD_TN), lambda l, j: (l, 0, j)),
        ],
        out_specs=pl.BlockSpec((None, 8, MOD_TN), lambda l, j: (l, 0, j)),
        compiler_params=_params("arbitrary", "arbitrary"),
        name="modulation",
    )(cond, w_mod, b_mod.reshape(DEPTH, 1, n))


IN_TM = 512
IN_TN = 512
GATE_PAD = 128


def _inproj_kernel(x_ref, mod_ref, w_ref, wg_ref, wgt_ref, bg_ref, bgt_ref, y_ref, g_ref, gt_ref):
    x = x_ref[...]
    h = _rms(x) * (1.0 + mod_ref[1:2, :]) + mod_ref[0:1, :]
    hb = h.astype(BF16)
    for n0 in range(0, D_PROJ, IN_TN):
        y_ref[:, n0:n0 + IN_TN] = _dot(hb, w_ref[:, n0:n0 + IN_TN])

    def gate_act(g, idx):
        log_sig = jnp.minimum(g, 0.0) - jnp.log(1.0 + jnp.exp(-jnp.abs(g)))
        return jnp.where((idx & 4) != 0, log_sig, g)

    g = _dot(hb, wg_ref[...]) + bg_ref[...]
    g_ref[...] = gate_act(g, lax.broadcasted_iota(jnp.int32, g.shape, 1))
    gt = _dot_nt(wgt_ref[...], hb) + bgt_ref[...]
    gt_ref[...] = gate_act(gt, lax.broadcasted_iota(jnp.int32, gt.shape, 0))


def _inproj(x, mod_l, w_in, wg, wgt, bg, bgt, layer):
    return pl.pallas_call(
        _inproj_kernel,
        out_shape=(
            jax.ShapeDtypeStruct((T_ALL, D_PROJ), F32),
            jax.ShapeDtypeStruct((T_ALL, GATE_PAD), F32),
            jax.ShapeDtypeStruct((N_GATES, T_ALL), F32),
        ),
        grid=(T_ALL // IN_TM,),
        in_specs=[
            pl.BlockSpec((IN_TM, D_MODEL), lambda i: (i, 0)),
            pl.BlockSpec((None, 6, D_MODEL), lambda i: (_segment(i, IN_TM), 0, 0)),
            pl.BlockSpec((None, D_MODEL, D_PROJ), lambda i: (layer, 0, 0)),
            pl.BlockSpec((D_MODEL, GATE_PAD), lambda i: (0, 0)),
            pl.BlockSpec((N_GATES, D_MODEL), lambda i: (0, 0)),
            pl.BlockSpec((1, GATE_PAD), lambda i: (0, 0)),
            pl.BlockSpec((N_GATES, 1), lambda i: (0, 0)),
        ],
        out_specs=(
            pl.BlockSpec((IN_TM, D_PROJ), lambda i: (i, 0)),
            pl.BlockSpec((IN_TM, GATE_PAD), lambda i: (i, 0)),
            pl.BlockSpec((N_GATES, IN_TM), lambda i: (0, i)),
        ),
        compiler_params=_params("arbitrary"),
        name="inproj",
    )(x, mod_l, w_in, wg, wgt, bg, bgt)


CONV_CB = 128
CONV_ROWS = 4096


def _conv_taps(x, w_ref, taps):
    rows = x.shape[0]
    acc = None
    for w_row, delta, valid in taps:
        shifted = x if delta == 0 else pltpu.roll(x, (-delta) % rows, 0)
        term = shifted * w_ref[w_row:w_row + 1, :]
        if valid is not None:
            term = jnp.where(valid, term, 0.0)
        acc = term if acc is None else acc + term
    return acc


def _conv_kernel(x_ref, w_ref, o_ref):
    grp = pl.program_id(0)
    cblk = pl.program_id(1)
    scale = jnp.where(cblk >= (D_MLSTM // CONV_CB), DH ** -0.5, 1.0).astype(F32)

    def finish(y):
        o_ref[...] = (y * _sigmoid(y)) * scale

    @pl.when(grp == 0)
    def _ctx():
        x = x_ref[...]
        pos = lax.broadcasted_iota(jnp.int32, x.shape, 0) & (CTX_LEN - 1)
        taps = [(3, -1, pos >= 1), (4, 0, None), (5, 1, pos <= CTX_LEN - 2)]
        finish(_conv_taps(x, w_ref, taps))

    @pl.when(grp == 1)
    def _lat():
        x = x_ref[...]
        pos = lax.broadcasted_iota(jnp.int32, x.shape, 0) & (LAT_LEN - 1)
        r = pos >> 6
        c = pos & (GRID_W - 1)
        row_ok = {-1: r >= 1, 0: None, 1: r <= GRID_H - 2}
        col_ok = {-1: c >= 1, 0: None, 1: c <= GRID_W - 2}
        taps = []
        for di in (-1, 0, 1):
            for dj in (-1, 0, 1):
                ok = row_ok[di]
                if col_ok[dj] is not None:
                    ok = col_ok[dj] if ok is None else (ok & col_ok[dj])
                taps.append(((di + 1) * 3 + (dj + 1), di * GRID_W + dj, ok))
        finish(_conv_taps(x, w_ref, taps))


def _conv(y, conv_w9):
    qk_blk0 = COL_QK // CONV_CB
    return pl.pallas_call(
        _conv_kernel,
        out_shape=jax.ShapeDtypeStruct((T_ALL, 2 * D_MLSTM), F32),
        grid=(T_ALL // CONV_ROWS, 2 * D_MLSTM // CONV_CB),
        in_specs=[
            pl.BlockSpec((CONV_ROWS, CONV_CB), lambda g, j: (g, qk_blk0 + j)),
            pl.BlockSpec((9, CONV_CB), lambda g, j: (0, j)),
        ],
        out_specs=pl.BlockSpec((CONV_ROWS, CONV_CB), lambda g, j: (g, j)),
        compiler_params=_params("arbitrary", "arbitrary"),
        name="conv_silu",
    )(y, conv_w9)


def _mlstm_direction(d, q_ref, k_ref, v_ref, g_ref, gt_ref, h_ref, cext_ref, m_ref):
    row = lax.broadcasted_iota(jnp.int32, (CHUNK, CHUNK), 0)
    col = lax.broadcasted_iota(jnp.int32, (CHUNK, CHUNK), 1)
    mask = (col <= row) if d == 0 else (col >= row)
    tri = jnp.where(mask, 1.0, 0.0).astype(BF16)
    g = g_ref[...]
    gt = gt_ref[...]
    g_hi, g_lo = _split_bf16(g)
    gt_hi, gt_lo = _split_bf16(gt)
    cum_col = _dot(tri, g_hi) + _dot(tri, g_lo)
    cum_row = _dot_nt(gt_hi, tri) + _dot_nt(gt_lo, tri)
    ones = jnp.ones((CHUNK, DH), BF16)
    for h in range(N_HEADS):
        ci = d * 8 + h
        cf = d * 8 + 4 + h
        sl = slice(h * DH, (h + 1) * DH)
        q = q_ref[:, sl].astype(BF16)
        k = k_ref[:, sl]
        v_ext = jnp.concatenate([v_ref[:, sl].astype(BF16), ones], axis=1)
        b_col = cum_col[:, cf:cf + 1]
        i_col = g[:, ci:ci + 1]
        b_row = cum_row[cf:cf + 1, :]
        i_row = gt[ci:ci + 1, :]
        m_prev = m_ref[d * 4 + h:d * 4 + h + 1, 0:1]
        dmat = jnp.where(mask, b_col - b_row + i_row, NEG_INF)
        prior = b_col + m_prev
        m_t = jnp.maximum(prior, jnp.max(dmat, axis=1, keepdims=True))
        w_intra = jnp.exp(dmat - m_t)
        w_prior = jnp.exp(prior - m_t)
        s = _dot_nt(q, k.astype(BF16)) * w_intra
        cext = cext_ref[d, h]
        tot = w_prior * _dot(q, cext.astype(BF16)) + _dot(s.astype(BF16), v_ext)
        num = tot[:, :DH]
        den = tot[:, DH:]
        h_ref[:, sl] = num / jnp.maximum(jnp.abs(den), jnp.exp(-m_t))
        b_last = b_col[CHUNK - 1:CHUNK, :] if d == 0 else b_col[0:1, :]
        log_end = b_last - b_col + i_col
        m_new = jnp.maximum(b_last + m_prev, jnp.max(log_end, axis=0, keepdims=True))
        w_end = jnp.exp(log_end - m_new)
        decay = jnp.exp(b_last + m_prev - m_new)
        cext_ref[d, h] = decay * cext + _dot_tn((k * w_end).astype(BF16), v_ext)
        m_ref[d * 4 + h:d * 4 + h + 1, :] = jnp.broadcast_to(m_new, (1, DH))


def _mlstm_kernel(qf_ref, kf_ref, vf_ref, gf_ref, gtf_ref, qb_ref, kb_ref, vb_ref, gb_ref, gtb_ref,
                  c0_ref, n0_ref, m0_ref, hf_ref, hb_ref, cn_ref, nn_ref, mn_ref, cext_ref, m_ref):
    c = pl.program_id(1)

    @pl.when(c == 0)
    def _init():
        for d in range(2):
            for h in range(N_HEADS):
                cext_ref[d, h, :, 0:DH] = c0_ref[d, h]
                cext_ref[d, h, :, DH:2 * DH] = jnp.broadcast_to(n0_ref[d, h], (DH, DH))
        m_ref[...] = m0_ref[...]

    _mlstm_direction(0, qf_ref, kf_ref, vf_ref, gf_ref, gtf_ref, hf_ref, cext_ref, m_ref)
    _mlstm_direction(1, qb_ref, kb_ref, vb_ref, gb_ref, gtb_ref, hb_ref, cext_ref, m_ref)

    @pl.when(c == pl.num_programs(1) - 1)
    def _final():
        for d in range(2):
            for h in range(N_HEADS):
                cn_ref[d, h] = cext_ref[d, h, :, 0:DH]
                nn_ref[d, h] = cext_ref[d, h, :, DH:DH + 1]
        mn_ref[...] = m_ref[...]


def _mlstm(qk, y, g, gt, c0, n0, m0, n_seq, seq_len, tok0):
    nc = seq_len // CHUNK
    blk0 = tok0 // CHUNK
    vcol = COL_V // D_MLSTM

    def chunk_specs(chunk_of):
        tok = lambda b, c: blk0 + b * nc + chunk_of(c)
        return [
            pl.BlockSpec((CHUNK, D_MLSTM), lambda b, c: (tok(b, c), 0)),
            pl.BlockSpec((CHUNK, D_MLSTM), lambda b, c: (tok(b, c), 1)),
            pl.BlockSpec((CHUNK, D_MLSTM), lambda b, c: (tok(b, c), vcol)),
            pl.BlockSpec((CHUNK, GATE_PAD), lambda b, c: (tok(b, c), 0)),
            pl.BlockSpec((N_GATES, CHUNK), lambda b, c: (0, tok(b, c))),
        ]

    fwd_chunk = lambda c: c
    bwd_chunk = lambda c: nc - 1 - c
    state_specs = [
        pl.BlockSpec((None, 2, N_HEADS, DH, DH), lambda b, c: (b, 0, 0, 0, 0)),
        pl.BlockSpec((None, 2, N_HEADS, DH, 1), lambda b, c: (b, 0, 0, 0, 0)),
        pl.BlockSpec((None, 2 * N_HEADS, DH), lambda b, c: (b, 0, 0)),
    ]
    n_tok = n_seq * seq_len
    return pl.pallas_call(
        _mlstm_kernel,
        out_shape=(
            jax.ShapeDtypeStruct((n_tok, D_MLSTM), F32),
            jax.ShapeDtypeStruct((n_tok, D_MLSTM), F32),
            jax.ShapeDtypeStruct((n_seq, 2, N_HEADS, DH, DH), F32),
            jax.ShapeDtypeStruct((n_seq, 2, N_HEADS, DH, 1), F32),
            jax.ShapeDtypeStruct((n_seq, 2 * N_HEADS, DH), F32),
        ),
        grid=(n_seq, nc),
        in_specs=chunk_specs(fwd_chunk) + chunk_specs(bwd_chunk) + state_specs,
        out_specs=(
            pl.BlockSpec((CHUNK, D_MLSTM), lambda b, c: (b * nc + fwd_chunk(c), 0)),
            pl.BlockSpec((CHUNK, D_MLSTM), lambda b, c: (b * nc + bwd_chunk(c), 0)),
            state_specs[0], state_specs[1], state_specs[2],
        ),
        scratch_shapes=[pltpu.VMEM((2, N_HEADS, DH, 2 * DH), F32), pltpu.VMEM((2 * N_HEADS, DH), F32)],
        compiler_params=_params("arbitrary", "arbitrary"),
        name="mlstm_scan",
    )(qk, qk, y, g, gt, qk, qk, y, g, gt, c0, n0, m0)


MIX_TM = 512


def _mix_kernel(x_ref, u_ref, vs_ref, o_ref, hfc_ref, hbc_ref, hfl_ref, hbl_ref, mod_ref, ws_ref, bs_ref,
                nw_ref, wout_ref, xo_ref, h2_ref, h2t_ref, cat_ref):
    is_ctx = pl.program_id(0) < T_CTX // MIX_TM
    for c0 in range(0, MIX_TM, CHUNK):
        rows = slice(c0, c0 + CHUNK)
        for h in range(N_HEADS):
            sl = slice(h * DH, (h + 1) * DH)
            vh = _rms(vs_ref[rows, sl]).astype(BF16)
            mixed = _dot(ws_ref[h], vh) + bs_ref[h]
            cat_ref[rows, sl] = (u_ref[rows, sl] * mixed).astype(BF16)
            h_sum = jnp.where(is_ctx, hfc_ref[rows, sl] + hbc_ref[rows, sl], hfl_ref[rows, sl] + hbl_ref[rows, sl])
            hm = _rms(h_sum) * nw_ref[:, sl]
            cat_ref[rows, D_GMLP + h * DH:D_GMLP + (h + 1) * DH] = (
                hm * _sigmoid(o_ref[rows, sl])).astype(BF16)
    mix = _dot(cat_ref[...], wout_ref[...])
    x_new = x_ref[...] + mod_ref[2:3, :] * mix
    xo_ref[...] = x_new
    h2 = (_rms(x_new) * (1.0 + mod_ref[4:5, :]) + mod_ref[3:4, :]).astype(BF16)
    h2_ref[...] = h2
    h2t_ref[...] = h2.T


def _mix(x, y, hf_c, hb_c, hf_l, hb_l, mod_l, w_s, bs_b, nw, w_out, layer):
    blk = lambda col: (lambda i: (i, col))
    n_ctx = T_CTX // MIX_TM
    ctx_blk = lambda i: (jnp.minimum(i, n_ctx - 1), 0)
    lat_blk = lambda i: (jnp.maximum(i - n_ctx, 0), 0)
    return pl.pallas_call(
        _mix_kernel,
        out_shape=(
            jax.ShapeDtypeStruct((T_ALL, D_MODEL), F32),
            jax.ShapeDtypeStruct((T_ALL, D_MODEL), BF16),
            jax.ShapeDtypeStruct((D_MODEL, T_ALL), BF16),
        ),
        grid=(T_ALL // MIX_TM,),
        in_specs=[
            pl.BlockSpec((MIX_TM, D_MODEL), blk(0)),
            pl.BlockSpec((MIX_TM, D_GMLP), blk(COL_U // D_GMLP)),
            pl.BlockSpec((MIX_TM, D_GMLP), blk(COL_VS // D_GMLP)),
            pl.BlockSpec((MIX_TM, D_MLSTM), blk(COL_O // D_MLSTM)),
            pl.BlockSpec((MIX_TM, D_MLSTM), ctx_blk),
            pl.BlockSpec((MIX_TM, D_MLSTM), ctx_blk),
            pl.BlockSpec((MIX_TM, D_MLSTM), lat_blk),
            pl.BlockSpec((MIX_TM, D_MLSTM), lat_blk),
            pl.BlockSpec((None, 6, D_MODEL), lambda i: (_segment(i, MIX_TM), 0, 0)),
            pl.BlockSpec((None, N_HEADS, CHUNK, CHUNK), lambda i: (layer, 0, 0, 0)),
            pl.BlockSpec((N_HEADS, CHUNK, DH), lambda i: (0, 0, 0)),
            pl.BlockSpec((1, D_MLSTM), lambda i: (0, 0)),
            pl.BlockSpec((None, D_MODEL, D_MODEL), lambda i: (layer, 0, 0)),
        ],
        out_specs=(
            pl.BlockSpec((MIX_TM, D_MODEL), blk(0)),
            pl.BlockSpec((MIX_TM, D_MODEL), blk(0)),
            pl.BlockSpec((D_MODEL, MIX_TM), lambda i: (0, i)),
        ),
        scratch_shapes=[pltpu.VMEM((MIX_TM, D_MODEL), BF16)],
        compiler_params=_params("arbitrary"),
        name="mixer_out",
    )(x, y, y, y, hf_c, hb_c, hf_l, hb_l, mod_l, w_s, bs_b, nw, w_out)


SEL_TM = 1024
_PAIR_COLS = [PEER_TOPK // (i + 1) for i in range(PEER_TOPK)]


LANES = 128
SUBLANES = 8


def _batcher_pairs(n):
    pairs = []
    p = 1
    while p < n:
        k = p
        while k >= 1:
            for j in range(k % p, n - k, 2 * k):
                for i in range(min(k, n - j - k)):
                    if (i + j) // (2 * p) == (i + j + k) // (2 * p):
                        pairs.append((i + j, i + j + k))
            k //= 2
        p *= 2
    return pairs


_SORT16 = tuple(_batcher_pairs(PEER_TOPK))


def _compare_exchange(v, a, b):
    x, y = v[a], v[b]
    if y is None:
        return
    if x is None:
        v[a], v[b] = y, None
        return
    v[a], v[b] = jnp.maximum(x, y), jnp.minimum(x, y)


def _top16_sorted(tiles):
    v = list(tiles) + [None] * (PEER_TOPK - len(tiles))
    for a, b in _SORT16:
        _compare_exchange(v, a, b)
    for shift in (4, 2, 1):
        other = [None if x is None else pltpu.roll(x, shift, 0) for x in v]
        merged = []
        for i in range(PEER_TOPK):
            x, y = v[i], other[PEER_TOPK - 1 - i]
            merged.append(y if x is None else x if y is None else jnp.maximum(x, y))
        v = merged
        dist = PEER_TOPK // 2
        while dist >= 1:
            for i in range(PEER_TOPK):
                if (i & dist) == 0:
                    _compare_exchange(v, i, i + dist)
            dist //= 2
    return v


def _rows_to_tile(rows, sub):
    out = rows[0]
    for r in range(1, SUBLANES):
        out = jnp.where(sub == r, rows[r], out)
    return out


def _select_chunk(s1_tiles, s2_tiles):
    v1 = _top16_sorted(s1_tiles)
    v2 = _top16_sorted(s2_tiles)
    sub = lax.broadcasted_iota(jnp.int32, (SUBLANES, LANES), 0)
    v2_lo = _rows_to_tile(v2[0:8], sub)
    v2_hi = _rows_to_tile(v2[8:16], sub)
    v1_hi = _rows_to_tile(v1[8:16], sub)
    cands = [v1[0] + v2_lo, v1[0] + v2_hi]
    for i in range(1, 8):
        cands.append(jnp.where(sub < _PAIR_COLS[i], v1[i] + v2_lo, NEG_INF))
    cands.append(v1_hi + v2[0])
    tau = _top16_sorted(cands)[PEER_TOPK - 1]
    best = v1[0] + v2[0]
    z_tile = None
    count_rank = []
    for idx, c in enumerate(cands[:9]):
        sel = c >= tau
        e = jnp.where(sel, jnp.exp(c - best), 0.0)
        z_tile = e if z_tile is None else z_tile + e
        n = jnp.where(sel, 1.0, 0.0)
        if idx == 1:
            count_rank[0] = count_rank[0] + n
        else:
            count_rank.append(n)
    z_tile = z_tile + jnp.where(cands[9] >= tau, jnp.exp(cands[9] - best), 0.0)
    inv_z = 1.0 / jnp.broadcast_to(jnp.sum(z_tile, axis=0, keepdims=True), (SUBLANES, LANES))
    count_rank = [jnp.broadcast_to(jnp.sum(n, axis=0, keepdims=True), (SUBLANES, LANES)) for n in count_rank]
    rank2, p2, count1, p1 = [], [], [], []
    for s1, s2 in zip(s1_tiles, s2_tiles):
        n = jnp.where(s1 + v2[0] >= tau, 1.0, 0.0)
        for i in range(8):
            n = jnp.where(s1 == v1[i], count_rank[i], n)
        count1.append(n)
        p1.append(jnp.exp(s1 - v1[0]) * inv_z)
        r = jnp.full((SUBLANES, LANES), float(PEER_TOPK), F32)
        for jr in reversed(range(PEER_TOPK)):
            r = jnp.where(s2 == v2[jr], float(jr), r)
        rank2.append(r)
        p2.append(jnp.exp(s2 - v2[0]))
    return rank2, p2, count1, p1


def _peer_sel_kernel(h_ref, wq_ref, keys_ref, r2_ref, p2_ref, n1_ref, p1_ref, sc_ref):
    q = _dot(h_ref[...], wq_ref[...])
    for p in range(2):
        sc_ref[p] = _dot_nt(keys_ref[p], q[:, p * 128:(p + 1) * 128].astype(BF16))

    def chunk(j, carry):
        lanes = pl.ds(pl.multiple_of(j * LANES, LANES), LANES)
        tile = lambda p, k: sc_ref[p, k * SUBLANES:(k + 1) * SUBLANES, lanes]
        s1 = [tile(0, k) for k in range(N_KEYS // SUBLANES)]
        s2 = [tile(1, k) for k in range(N_KEYS // SUBLANES)]
        rank2, p2, count1, p1 = _select_chunk(s1, s2)
        for k in range(N_KEYS // SUBLANES):
            rows = slice(k * SUBLANES, (k + 1) * SUBLANES)
            n1_ref[j, rows, :] = count1[k]
            p1_ref[j, rows, :] = p1[k]
        for k in range(0, N_KEYS // SUBLANES, 2):
            rows = slice((k // 2) * SUBLANES, (k // 2 + 1) * SUBLANES)
            r2_ref[j, rows, :] = pltpu.bitcast(jnp.concatenate(rank2[k:k + 2], axis=0).astype(BF16), jnp.uint32)
            p2_ref[j, rows, :] = pltpu.bitcast(jnp.concatenate(p2[k:k + 2], axis=0).astype(BF16), jnp.uint32)
        return carry

    lax.fori_loop(0, SEL_TM // LANES, chunk, 0)


def _peer_select(h2, wq, keys, layer):
    tab = jax.ShapeDtypeStruct((PEER_HEADS, T_ALL // LANES, N_KEYS, LANES), F32)
    tab16 = jax.ShapeDtypeStruct((PEER_HEADS, T_ALL // LANES, N_KEYS // 2, LANES), jnp.uint32)
    tab_spec = pl.BlockSpec((None, SEL_TM // LANES, N_KEYS, LANES), lambda i, h: (h, i, 0, 0))
    tab16_spec = pl.BlockSpec((None, SEL_TM // LANES, N_KEYS // 2, LANES), lambda i, h: (h, i, 0, 0))
    return pl.pallas_call(
        _peer_sel_kernel,
        out_shape=(tab16, tab16, tab, tab),
        grid=(T_ALL // SEL_TM, PEER_HEADS),
        in_specs=[
            pl.BlockSpec((SEL_TM, D_MODEL), lambda i, h: (i, 0)),
            pl.BlockSpec((None, D_MODEL, 2 * 128), lambda i, h: (layer, 0, h)),
            pl.BlockSpec((None, None, 2, N_KEYS, 128), lambda i, h: (layer, h, 0, 0, 0)),
        ],
        out_specs=(tab16_spec, tab16_spec, tab_spec, tab_spec),
        scratch_shapes=[pltpu.VMEM((2, N_KEYS, SEL_TM), F32)],
        compiler_params=_params("arbitrary", "arbitrary"),
        name="peer_select",
    )(h2, wq, keys)


EXP_TM = 512
EXP_KEY_ROWS = 8
EXP_TE = EXP_KEY_ROWS * N_KEYS
EXP_STEPS = N_EXPERTS // EXP_TE
EXP_CHUNK_ROWS = 2
EXP_CHUNK = EXP_CHUNK_ROWS * N_KEYS
EXP_CHUNKS = EXP_TE // EXP_CHUNK


def _peer_expert_kernel(ht_ref, u_ref, unext_ref, vt_ref, r2_ref, p2_ref, n1_ref, p1_ref,
                        x_ref, mod_ref, o_ref, act_ref, gate_ref, acc_ref):
    j = pl.program_id(1)

    @pl.when(j == 0)
    def _head():
        acc_ref[...] = jnp.zeros_like(acc_ref)
        act_ref[0] = _dot(u_ref[0:EXP_CHUNK, :], ht_ref[...])

    part = None
    for c in range(EXP_CHUNKS):
        anchor = None
        for lc in range(EXP_TM // LANES):
            lanes = slice(lc * LANES, (lc + 1) * LANES)
            w = [None] * EXP_CHUNK_ROWS
            for h in range(PEER_HEADS):
                rank2 = pltpu.bitcast(r2_ref[h, lc], BF16).reshape(N_KEYS // 16, 16, LANES)
                p2 = pltpu.bitcast(p2_ref[h, lc], BF16).reshape(N_KEYS // 16, 16, LANES)
                for k in range(EXP_CHUNK_ROWS):
                    r = EXP_CHUNK_ROWS * c + k
                    count1 = jnp.broadcast_to(n1_ref[h, lc, r:r + 1, :], (16, LANES)).astype(BF16)
                    p1 = jnp.broadcast_to(p1_ref[h, lc, r:r + 1, :], (16, LANES)).astype(BF16)
                    term = jnp.where(rank2 < count1[None], p2, jnp.zeros_like(p2)) * p1[None]
                    w[k] = term if w[k] is None else w[k] + term
            for k in range(EXP_CHUNK_ROWS):
                rows = slice(k * N_KEYS, (k + 1) * N_KEYS)
                act = act_ref[c, rows, lanes].reshape(N_KEYS // 16, 16, LANES)
                g = (w[k] * _gelu_tanh(act).astype(BF16)).reshape(N_KEYS, LANES)
                gate_ref[c, rows, lanes] = g
                if anchor is None:
                    anchor = g[0:16, :]
        zero = pltpu.bitcast((pltpu.bitcast(anchor, jnp.uint32) >> 16) >> 16, BF16)
        nxt = unext_ref[...] if c == EXP_CHUNKS - 1 else u_ref[(c + 1) * EXP_CHUNK:(c + 2) * EXP_CHUNK, :]
        nxt = nxt + jnp.tile(zero, (EXP_CHUNK // 16, D_MODEL // LANES))
        act_ref[(c + 1) % EXP_CHUNKS] = _dot(nxt, ht_ref[...])
        d = _dot(vt_ref[:, c * EXP_CHUNK:(c + 1) * EXP_CHUNK], gate_ref[c])
        part = d if part is None else part + d
    acc_ref[...] += part

    @pl.when(j == EXP_STEPS - 1)
    def _tail():
        o_ref[...] = x_ref[...] + mod_ref[5:6, :] * acc_ref[...].T


def _peer_experts(h2t, u, vt, r2, p2, n1, p1, x, mod_l, layer):
    n_chunks = N_EXPERTS // EXP_CHUNK
    tab_spec = pl.BlockSpec((PEER_HEADS, EXP_TM // LANES, N_KEYS // 2, LANES), lambda i, j: (0, i, 0, 0))
    row_spec = pl.BlockSpec((PEER_HEADS, EXP_TM // LANES, EXP_KEY_ROWS, LANES), lambda i, j: (0, i, j, 0))
    return pl.pallas_call(
        _peer_expert_kernel,
        out_shape=jax.ShapeDtypeStruct((T_ALL, D_MODEL), F32),
        grid=(T_ALL // EXP_TM, EXP_STEPS),
        in_specs=[
            pl.BlockSpec((D_MODEL, EXP_TM), lambda i, j: (0, i)),
            pl.BlockSpec((None, EXP_TE, D_MODEL), lambda i, j: (layer, j, 0)),
            pl.BlockSpec((None, EXP_CHUNK, D_MODEL),
                         lambda i, j: (layer, jnp.minimum((j + 1) * EXP_CHUNKS, n_chunks - 1), 0)),
            pl.BlockSpec((None, D_MODEL, EXP_TE), lambda i, j: (layer, 0, j)),
            tab_spec, tab_spec, row_spec, row_spec,
            pl.BlockSpec((EXP_TM, D_MODEL), lambda i, j: (i, 0)),
            pl.BlockSpec((None, 6, D_MODEL), lambda i, j: (_segment(i, EXP_TM), 0, 0)),
        ],
        out_specs=pl.BlockSpec((EXP_TM, D_MODEL), lambda i, j: (i, 0)),
        scratch_shapes=[
            pltpu.VMEM((EXP_CHUNKS, EXP_CHUNK, EXP_TM), F32),
            pltpu.VMEM((EXP_CHUNKS, EXP_CHUNK, EXP_TM), BF16),
            pltpu.VMEM((D_MODEL, EXP_TM), F32),
        ],
        compiler_params=_params("arbitrary", "arbitrary"),
        name="peer_experts",
    )(h2t, u, u, vt, r2, p2, n1, p1, x, mod_l)


FIN_TM = 1024


def _final_kernel(x_ref, w_ref, o_ref):
    o_ref[...] = _rms(x_ref[...]) * w_ref[...]


def _final_norm(x, w, tok0, n_tok):
    blk0 = tok0 // FIN_TM
    return pl.pallas_call(
        _final_kernel,
        out_shape=jax.ShapeDtypeStruct((n_tok, D_MODEL), F32),
        grid=(n_tok // FIN_TM,),
        in_specs=[pl.BlockSpec((FIN_TM, D_MODEL), lambda i: (blk0 + i, 0)),
                  pl.BlockSpec((1, D_MODEL), lambda i: (0, 0))],
        out_specs=pl.BlockSpec((FIN_TM, D_MODEL), lambda i: (i, 0)),
        compiler_params=_params("arbitrary"),
        name="final_norm",
    )(x, w)


def kernel(x_prompt, x_sample, state_C, state_n, state_m, c, c_ctx, w_mod, b_mod, w_in, b_gate,
           conv_w, w_s, b_s, mlstm_norm_w, w_out, peer_wq, peer_keys, peer_u, peer_v, final_norm_w):
    assert x_prompt.shape == (N_CTX_SEQ, CTX_LEN, D_MODEL)
    assert x_sample.shape == (N_LAT_SEQ, LAT_LEN, D_MODEL)
    x = jnp.concatenate([x_prompt.reshape(T_CTX, D_MODEL), x_sample.reshape(T_LAT, D_MODEL)], axis=0)

    cond = jnp.zeros((8, D_MODEL), F32).at[0].set(c_ctx).at[1:1 + N_LAT_SEQ].set(c)
    mod = _modulation(cond, w_mod, b_mod)[:, :1 + N_LAT_SEQ].reshape(DEPTH, 1 + N_LAT_SEQ, 6, D_MODEL)

    zero_c = jnp.zeros((N_CTX_SEQ, 2, N_HEADS, DH, DH), F32)
    zero_n = jnp.zeros((N_CTX_SEQ, 2, N_HEADS, DH, 1), F32)
    zero_m = jnp.zeros((N_CTX_SEQ, 2 * N_HEADS, DH), F32)

    w_in_b = w_in.astype(BF16)
    w_s_b = w_s.astype(BF16)
    w_out_b = w_out.astype(BF16)
    wq_b = peer_wq.astype(BF16)
    keys_b = peer_keys.astype(BF16)
    u_b = peer_u.astype(BF16)
    vt_b = jnp.swapaxes(peer_v, 1, 2).astype(BF16)

    new_c, new_n, new_m = [], [], []
    for l in range(DEPTH):
        w_gate = w_in[l, :, D_PROJ:]
        wg = jnp.pad(w_gate, ((0, 0), (0, GATE_PAD - N_GATES))).astype(BF16)
        wgt = w_gate.T.astype(BF16)
        bg = jnp.pad(b_gate[l].reshape(1, N_GATES), ((0, 0), (0, GATE_PAD - N_GATES)))
        bgt = b_gate[l].reshape(N_GATES, 1)
        y, g, gt = _inproj(x, mod[l], w_in_b, wg, wgt, bg, bgt, l)

        qk = _conv(y, conv_w[l].reshape(9, 2 * D_MLSTM))

        hf_c, hb_c, c_ctx_new, n_ctx_new, m_ctx_new = _mlstm(
            qk, y, g, gt, zero_c, zero_n, zero_m, N_CTX_SEQ, CTX_LEN, 0)
        lat_m0 = jnp.broadcast_to(state_m[:, l].reshape(N_LAT_SEQ, 2 * N_HEADS, 1),
                                  (N_LAT_SEQ, 2 * N_HEADS, DH))
        hf_l, hb_l, _, _, _ = _mlstm(
            qk, y, g, gt, state_C[:, l], state_n[:, l][..., None], lat_m0, N_LAT_SEQ, LAT_LEN, T_CTX)
        new_c.append(c_ctx_new)
        new_n.append(n_ctx_new[..., 0])
        new_m.append(m_ctx_new[..., 0].reshape(N_CTX_SEQ, 2, N_HEADS))

        bs_b = jnp.broadcast_to(b_s[l][:, :, None], (N_HEADS, CHUNK, DH))
        x, h2, h2t = _mix(x, y, hf_c, hb_c, hf_l, hb_l, mod[l], w_s_b, bs_b,
                          mlstm_norm_w[l].reshape(1, D_MLSTM), w_out_b, l)

        rank2, p2, count1, p1 = _peer_select(h2, wq_b, keys_b, l)
        x = _peer_experts(h2t, u_b, vt_b, rank2, p2, count1, p1, x, mod[l], l)

    fin_w = final_norm_w.reshape(1, D_MODEL)
    y_prompt = _final_norm(x, fin_w, 0, T_CTX).reshape(N_CTX_SEQ, CTX_LEN, D_MODEL)
    y_sample = _final_norm(x, fin_w, T_CTX, T_LAT).reshape(N_LAT_SEQ, LAT_LEN, D_MODEL)
    return (y_prompt, y_sample, jnp.stack(new_c, axis=1), jnp.stack(new_n, axis=1),
            jnp.stack(new_m, axis=1))
```

```python
import jax
import jax.numpy as jnp
from jax import lax
from jax.experimental import pallas as pl
from jax.experimental.pallas import tpu as pltpu

F32 = jnp.float32
BF16 = jnp.bfloat16

D_MODEL = 1024
DEPTH = 4
N_CTX_SEQ, CTX_LEN = 16, 256
N_LAT_SEQ, LAT_LEN = 2, 2048
T_CTX = N_CTX_SEQ * CTX_LEN
T_LAT = N_LAT_SEQ * LAT_LEN
T_ALL = T_CTX + T_LAT
GRID_W = 64
GRID_H = LAT_LEN // GRID_W
CHUNK = 128
D_GMLP = 512
D_MLSTM = 512
N_HEADS = 4
DH = 128
N_KEYS = 128
N_EXPERTS = N_KEYS * N_KEYS
PEER_HEADS = 8
PEER_TOPK = 16
D_PROJ = 2 * D_GMLP + 4 * D_MLSTM
N_GATES = 16
EPS = 1e-6
NEG_INF = float("-inf")
POS_INF = float("inf")

VMEM_LIMIT_BYTES = 56 * 1024 * 1024

COL_U, COL_VS, COL_QK, COL_V, COL_O = 0, 512, 1024, 2048, 2560


def _dot(a, b):
    return jnp.dot(a, b, preferred_element_type=F32)


def _dot_nt(a, b):
    return lax.dot_general(a, b, (((1,), (1,)), ((), ())), preferred_element_type=F32)


def _dot_tn(a, b):
    return lax.dot_general(a, b, (((0,), (0,)), ((), ())), preferred_element_type=F32)


def _split_bf16(x):
    hi = x.astype(BF16)
    lo = (x - hi.astype(F32)).astype(BF16)
    return hi, lo


def _rms(x):
    return x * lax.rsqrt(jnp.mean(x * x, axis=-1, keepdims=True) + EPS)


def _sigmoid(x):
    return 1.0 / (1.0 + jnp.exp(-x))


def _gelu_tanh(x):
    return 0.5 * x * (1.0 + jnp.tanh(0.7978845608028654 * (x + 0.044715 * (x * x * x))))


def _segment(i, tile):
    start = i * tile
    return (start >= T_CTX).astype(jnp.int32) + (start >= T_CTX + LAT_LEN).astype(jnp.int32)


def _params(*sem):
    return pltpu.CompilerParams(dimension_semantics=sem, vmem_limit_bytes=VMEM_LIMIT_BYTES)


MOD_TN = 1536


def _mod_kernel(c_ref, w_ref, b_ref, o_ref):
    c = c_ref[...]
    a = (c * _sigmoid(c)).astype(BF16)
    o_ref[...] = _dot(a, w_ref[...].astype(BF16)) + b_ref[...]


def _modulation(cond, w_mod, b_mod):
    n = w_mod.shape[-1]
    return pl.pallas_call(
        _mod_kernel,
        out_shape=jax.ShapeDtypeStruct((DEPTH, 8, n), F32),
        grid=(DEPTH, n // MOD_TN),
        in_specs=[
            pl.BlockSpec((8, D_MODEL), lambda l, j: (0, 0)),
            pl.BlockSpec((None, D_MODEL, MOD_TN), lambda l, j: (l, 0, j)),
            pl.BlockSpec((None, 1, MOD_TN), lambda l, j: (l, 0, j)),
        ],
        out_specs=pl.BlockSpec((None, 8, MOD_TN), lambda l, j: (l, 0, j)),
        compiler_params=_params("arbitrary", "arbitrary"),
        name="modulation",
    )(cond, w_mod, b_mod.reshape(DEPTH, 1, n))


IN_TM = 512
IN_TN = 512
GATE_PAD = 128
LANES = 128
SUBLANES = 8
GATE_COLS = 8 * LANES


def _inproj_kernel(x_ref, mod_ref, w_ref, wg_ref, wgt_ref, bg_ref, bgt_ref, y_ref, g_ref, gt_ref):
    x = x_ref[...]
    h = _rms(x) * (1.0 + mod_ref[1:2, :]) + mod_ref[0:1, :]
    hb = h.astype(BF16)
    for n0 in range(0, D_PROJ, IN_TN):
        y_ref[:, n0:n0 + IN_TN] = _dot(hb, w_ref[:, n0:n0 + IN_TN])

    def gate_act(g, idx):
        log_sig = jnp.minimum(g, 0.0) - jnp.log(1.0 + jnp.exp(-jnp.abs(g)))
        return jnp.where((idx & 4) != 0, log_sig, g)

    g = _dot(hb, wg_ref[...]) + bg_ref[...]
    g = gate_act(g, lax.broadcasted_iota(jnp.int32, g.shape, 1))
    for c in range(N_GATES):
        g_ref[:, c * LANES:(c + 1) * LANES] = jnp.broadcast_to(g[:, c:c + 1], (IN_TM, LANES))
    gt = _dot_nt(wgt_ref[...], hb) + bgt_ref[...]
    gt_ref[...] = gate_act(gt, lax.broadcasted_iota(jnp.int32, gt.shape, 0))


def _inproj(x, mod_l, w_in, wg, wgt, bg, bgt, layer):
    return pl.pallas_call(
        _inproj_kernel,
        out_shape=(
            jax.ShapeDtypeStruct((T_ALL, D_PROJ), F32),
            jax.ShapeDtypeStruct((T_ALL, N_GATES * LANES), F32),
            jax.ShapeDtypeStruct((N_GATES, T_ALL), F32),
        ),
        grid=(T_ALL // IN_TM,),
        in_specs=[
            pl.BlockSpec((IN_TM, D_MODEL), lambda i: (i, 0)),
            pl.BlockSpec((None, 6, D_MODEL), lambda i: (_segment(i, IN_TM), 0, 0)),
            pl.BlockSpec((None, D_MODEL, D_PROJ), lambda i: (layer, 0, 0)),
            pl.BlockSpec((D_MODEL, GATE_PAD), lambda i: (0, 0)),
            pl.BlockSpec((N_GATES, D_MODEL), lambda i: (0, 0)),
            pl.BlockSpec((1, GATE_PAD), lambda i: (0, 0)),
            pl.BlockSpec((N_GATES, 1), lambda i: (0, 0)),
        ],
        out_specs=(
            pl.BlockSpec((IN_TM, D_PROJ), lambda i: (i, 0)),
            pl.BlockSpec((IN_TM, N_GATES * LANES), lambda i: (i, 0)),
            pl.BlockSpec((N_GATES, IN_TM), lambda i: (0, i)),
        ),
        compiler_params=_params("arbitrary"),
        name="inproj",
    )(x, mod_l, w_in, wg, wgt, bg, bgt)


CONV_CB = 128
CONV_ROWS = 4096


def _conv_taps(x, w_ref, taps):
    rows = x.shape[0]
    acc = None
    for w_row, delta, valid in taps:
        shifted = x if delta == 0 else pltpu.roll(x, (-delta) % rows, 0)
        term = shifted * w_ref[w_row:w_row + 1, :]
        if valid is not None:
            term = jnp.where(valid, term, 0.0)
        acc = term if acc is None else acc + term
    return acc


def _conv_kernel(x_ref, w_ref, o_ref):
    grp = pl.program_id(0)
    cblk = pl.program_id(1)
    scale = jnp.where(cblk >= (D_MLSTM // CONV_CB), DH ** -0.5, 1.0).astype(F32)

    def finish(y):
        o_ref[...] = (y * _sigmoid(y)) * scale

    @pl.when(grp == 0)
    def _ctx():
        x = x_ref[...]
        pos = lax.broadcasted_iota(jnp.int32, x.shape, 0) & (CTX_LEN - 1)
        taps = [(3, -1, pos >= 1), (4, 0, None), (5, 1, pos <= CTX_LEN - 2)]
        finish(_conv_taps(x, w_ref, taps))

    @pl.when(grp == 1)
    def _lat():
        x = x_ref[...]
        pos = lax.broadcasted_iota(jnp.int32, x.shape, 0) & (LAT_LEN - 1)
        r = pos >> 6
        c = pos & (GRID_W - 1)
        row_ok = {-1: r >= 1, 0: None, 1: r <= GRID_H - 2}
        col_ok = {-1: c >= 1, 0: None, 1: c <= GRID_W - 2}
        taps = []
        for di in (-1, 0, 1):
            for dj in (-1, 0, 1):
                ok = row_ok[di]
                if col_ok[dj] is not None:
                    ok = col_ok[dj] if ok is None else (ok & col_ok[dj])
                taps.append(((di + 1) * 3 + (dj + 1), di * GRID_W + dj, ok))
        finish(_conv_taps(x, w_ref, taps))


def _conv(y, conv_w9):
    qk_blk0 = COL_QK // CONV_CB
    return pl.pallas_call(
        _conv_kernel,
        out_shape=jax.ShapeDtypeStruct((T_ALL, 2 * D_MLSTM), F32),
        grid=(T_ALL // CONV_ROWS, 2 * D_MLSTM // CONV_CB),
        in_specs=[
            pl.BlockSpec((CONV_ROWS, CONV_CB), lambda g, j: (g, qk_blk0 + j)),
            pl.BlockSpec((9, CONV_CB), lambda g, j: (0, j)),
        ],
        out_specs=pl.BlockSpec((CONV_ROWS, CONV_CB), lambda g, j: (g, j)),
        compiler_params=_params("arbitrary", "arbitrary"),
        name="conv_silu",
    )(y, conv_w9)


def _mlstm_direction(d, q_ref, k_ref, v_ref, g_ref, gt_ref, h_ref, cext_ref, m_ref):
    row = lax.broadcasted_iota(jnp.int32, (CHUNK, CHUNK), 0)
    col = lax.broadcasted_iota(jnp.int32, (CHUNK, CHUNK), 1)
    mask = (col <= row) if d == 0 else (col >= row)
    tri = jnp.where(mask, 1.0, 0.0).astype(BF16)
    gt = gt_ref[...]
    gt_hi, gt_lo = _split_bf16(gt)
    cum_row = _dot_nt(gt_hi, tri) + _dot_nt(gt_lo, tri)
    lf_hi, lf_lo = _split_bf16(g_ref[:, N_HEADS * LANES:])
    cum_col = _dot(tri, lf_hi) + _dot(tri, lf_lo)
    ones = jnp.ones((CHUNK, DH), BF16)
    for h in range(N_HEADS):
        ci = d * 8 + h
        cf = d * 8 + 4 + h
        sl = slice(h * DH, (h + 1) * DH)
        q = q_ref[:, sl].astype(BF16)
        k = k_ref[:, sl]
        v_ext = jnp.concatenate([v_ref[:, sl].astype(BF16), ones], axis=1)
        b_col = cum_col[:, sl]
        i_col = g_ref[:, sl]
        b_row = cum_row[cf:cf + 1, :]
        i_row = gt[ci:ci + 1, :]
        m_prev = m_ref[d * 4 + h:d * 4 + h + 1, :]
        dmat = jnp.where(mask, b_col - b_row + i_row, NEG_INF)
        prior = b_col + m_prev
        row_max = jnp.broadcast_to(jnp.max(dmat, axis=1, keepdims=True), (CHUNK, LANES))
        m_t = jnp.maximum(prior, row_max)
        w_intra = jnp.exp(dmat - m_t)
        w_prior = jnp.exp(prior - m_t)
        s = _dot_nt(q, k.astype(BF16)) * w_intra
        cext = cext_ref[d, h]
        inter = _dot(q, cext.astype(BF16))
        intra = _dot(s.astype(BF16), v_ext)
        num = w_prior * inter[:, :DH] + intra[:, :DH]
        den = w_prior * inter[:, DH:] + intra[:, DH:]
        h_ref[:, sl] = num / jnp.maximum(jnp.abs(den), jnp.exp(-m_t))
        b_last = b_col[CHUNK - 1:CHUNK, :] if d == 0 else b_col[0:1, :]
        log_end = b_last - b_col + i_col
        m_new = jnp.maximum(b_last + m_prev, jnp.max(log_end, axis=0, keepdims=True))
        w_end = jnp.exp(log_end - m_new)
        decay = jnp.exp(b_last + m_prev - m_new)
        upd = _dot_tn((k * w_end).astype(BF16), v_ext)
        cext_ref[d, h, :, 0:DH] = decay * cext[:, :DH] + upd[:, :DH]
        cext_ref[d, h, :, DH:2 * DH] = decay * cext[:, DH:] + upd[:, DH:]
        m_ref[d * 4 + h:d * 4 + h + 1, :] = m_new


def _mlstm_kernel(qf_ref, kf_ref, vf_ref, gf_ref, gtf_ref, qb_ref, kb_ref, vb_ref, gb_ref, gtb_ref,
                  c0_ref, n0_ref, m0_ref, hf_ref, hb_ref, cn_ref, nn_ref, mn_ref, cext_ref, m_ref):
    c = pl.program_id(1)

    @pl.when(c == 0)
    def _init():
        for d in range(2):
            for h in range(N_HEADS):
                cext_ref[d, h, :, 0:DH] = c0_ref[d, h]
                cext_ref[d, h, :, DH:2 * DH] = jnp.broadcast_to(n0_ref[d, h], (DH, DH))
        m_ref[...] = m0_ref[...]

    _mlstm_direction(0, qf_ref, kf_ref, vf_ref, gf_ref, gtf_ref, hf_ref, cext_ref, m_ref)
    _mlstm_direction(1, qb_ref, kb_ref, vb_ref, gb_ref, gtb_ref, hb_ref, cext_ref, m_ref)

    @pl.when(c == pl.num_programs(1) - 1)
    def _final():
        for d in range(2):
            for h in range(N_HEADS):
                cn_ref[d, h] = cext_ref[d, h, :, 0:DH]
                nn_ref[d, h] = cext_ref[d, h, :, DH:DH + 1]
        mn_ref[...] = m_ref[...]


def _mlstm(qk, y, g, gt, c0, n0, m0, n_seq, seq_len, tok0):
    nc = seq_len // CHUNK
    blk0 = tok0 // CHUNK
    vcol = COL_V // D_MLSTM

    def chunk_specs(chunk_of, direction):
        tok = lambda b, c: blk0 + b * nc + chunk_of(c)
        return [
            pl.BlockSpec((CHUNK, D_MLSTM), lambda b, c: (tok(b, c), 0)),
            pl.BlockSpec((CHUNK, D_MLSTM), lambda b, c: (tok(b, c), 1)),
            pl.BlockSpec((CHUNK, D_MLSTM), lambda b, c: (tok(b, c), vcol)),
            pl.BlockSpec((CHUNK, GATE_COLS), lambda b, c: (tok(b, c), direction)),
            pl.BlockSpec((N_GATES, CHUNK), lambda b, c: (0, tok(b, c))),
        ]

    fwd_chunk = lambda c: c
    bwd_chunk = lambda c: nc - 1 - c
    state_specs = [
        pl.BlockSpec((None, 2, N_HEADS, DH, DH), lambda b, c: (b, 0, 0, 0, 0)),
        pl.BlockSpec((None, 2, N_HEADS, DH, 1), lambda b, c: (b, 0, 0, 0, 0)),
        pl.BlockSpec((None, 2 * N_HEADS, DH), lambda b, c: (b, 0, 0)),
    ]
    n_tok = n_seq * seq_len
    return pl.pallas_call(
        _mlstm_kernel,
        out_shape=(
            jax.ShapeDtypeStruct((n_tok, D_MLSTM), F32),
            jax.ShapeDtypeStruct((n_tok, D_MLSTM), F32),
            jax.ShapeDtypeStruct((n_seq, 2, N_HEADS, DH, DH), F32),
            jax.ShapeDtypeStruct((n_seq, 2, N_HEADS, DH, 1), F32),
            jax.ShapeDtypeStruct((n_seq, 2 * N_HEADS, DH), F32),
        ),
        grid=(n_seq, nc),
        in_specs=chunk_specs(fwd_chunk, 0) + chunk_specs(bwd_chunk, 1) + state_specs,
        out_specs=(
            pl.BlockSpec((CHUNK, D_MLSTM), lambda b, c: (b * nc + fwd_chunk(c), 0)),
            pl.BlockSpec((CHUNK, D_MLSTM), lambda b, c: (b * nc + bwd_chunk(c), 0)),
            state_specs[0], state_specs[1], state_specs[2],
        ),
        scratch_shapes=[pltpu.VMEM((2, N_HEADS, DH, 2 * DH), F32), pltpu.VMEM((2 * N_HEADS, DH), F32)],
        compiler_params=_params("arbitrary", "arbitrary"),
        name="mlstm_scan",
    )(qk, qk, y, g, gt, qk, qk, y, g, gt, c0, n0, m0)


MIX_TM = 512


def _mix_kernel(x_ref, u_ref, vs_ref, o_ref, hfc_ref, hbc_ref, hfl_ref, hbl_ref, mod_ref, ws_ref, bs_ref,
                nw_ref, wout_ref, xo_ref, h2_ref, h2t_ref, cat_ref):
    is_ctx = pl.program_id(0) < T_CTX // MIX_TM
    for c0 in range(0, MIX_TM, CHUNK):
        rows = slice(c0, c0 + CHUNK)
        for h in range(N_HEADS):
            sl = slice(h * DH, (h + 1) * DH)
            vh = _rms(vs_ref[rows, sl]).astype(BF16)
            mixed = _dot(ws_ref[h], vh) + bs_ref[h]
            cat_ref[rows, sl] = (u_ref[rows, sl] * mixed).astype(BF16)
            h_sum = jnp.where(is_ctx, hfc_ref[rows, sl] + hbc_ref[rows, sl], hfl_ref[rows, sl] + hbl_ref[rows, sl])
            hm = _rms(h_sum) * nw_ref[:, sl]
            cat_ref[rows, D_GMLP + h * DH:D_GMLP + (h + 1) * DH] = (
                hm * _sigmoid(o_ref[rows, sl])).astype(BF16)
    mix = _dot(cat_ref[...], wout_ref[...])
    x_new = x_ref[...] + mod_ref[2:3, :] * mix
    xo_ref[...] = x_new
    h2 = (_rms(x_new) * (1.0 + mod_ref[4:5, :]) + mod_ref[3:4, :]).astype(BF16)
    h2_ref[...] = h2
    h2t_ref[...] = h2.T


def _mix(x, y, hf_c, hb_c, hf_l, hb_l, mod_l, w_s, bs_b, nw, w_out, layer):
    blk = lambda col: (lambda i: (i, col))
    n_ctx = T_CTX // MIX_TM
    ctx_blk = lambda i: (jnp.minimum(i, n_ctx - 1), 0)
    lat_blk = lambda i: (jnp.maximum(i - n_ctx, 0), 0)
    return pl.pallas_call(
        _mix_kernel,
        out_shape=(
            jax.ShapeDtypeStruct((T_ALL, D_MODEL), F32),
            jax.ShapeDtypeStruct((T_ALL, D_MODEL), BF16),
            jax.ShapeDtypeStruct((D_MODEL, T_ALL), BF16),
        ),
        grid=(T_ALL // MIX_TM,),
        in_specs=[
            pl.BlockSpec((MIX_TM, D_MODEL), blk(0)),
            pl.BlockSpec((MIX_TM, D_GMLP), blk(COL_U // D_GMLP)),
            pl.BlockSpec((MIX_TM, D_GMLP), blk(COL_VS // D_GMLP)),
            pl.BlockSpec((MIX_TM, D_MLSTM), blk(COL_O // D_MLSTM)),
            pl.BlockSpec((MIX_TM, D_MLSTM), ctx_blk),
            pl.BlockSpec((MIX_TM, D_MLSTM), ctx_blk),
            pl.BlockSpec((MIX_TM, D_MLSTM), lat_blk),
            pl.BlockSpec((MIX_TM, D_MLSTM), lat_blk),
            pl.BlockSpec((None, 6, D_MODEL), lambda i: (_segment(i, MIX_TM), 0, 0)),
            pl.BlockSpec((None, N_HEADS, CHUNK, CHUNK), lambda i: (layer, 0, 0, 0)),
            pl.BlockSpec((N_HEADS, CHUNK, DH), lambda i: (0, 0, 0)),
            pl.BlockSpec((1, D_MLSTM), lambda i: (0, 0)),
            pl.BlockSpec((None, D_MODEL, D_MODEL), lambda i: (layer, 0, 0)),
        ],
        out_specs=(
            pl.BlockSpec((MIX_TM, D_MODEL), blk(0)),
            pl.BlockSpec((MIX_TM, D_MODEL), blk(0)),
            pl.BlockSpec((D_MODEL, MIX_TM), lambda i: (0, i)),
        ),
        scratch_shapes=[pltpu.VMEM((MIX_TM, D_MODEL), BF16)],
        compiler_params=_params("arbitrary"),
        name="mixer_out",
    )(x, y, y, y, hf_c, hb_c, hf_l, hb_l, mod_l, w_s, bs_b, nw, w_out)


SEL_TM = 1024
_PAIR_COLS = [PEER_TOPK // (i + 1) for i in range(PEER_TOPK)]


def _batcher_pairs(n):
    pairs = []
    p = 1
    while p < n:
        k = p
        while k >= 1:
            for j in range(k % p, n - k, 2 * k):
                for i in range(min(k, n - j - k)):
                    if (i + j) // (2 * p) == (i + j + k) // (2 * p):
                        pairs.append((i + j, i + j + k))
            k //= 2
        p *= 2
    return pairs


_SORT16 = tuple(_batcher_pairs(PEER_TOPK))


def _compare_exchange(v, a, b):
    x, y = v[a], v[b]
    if y is None:
        return
    if x is None:
        v[a], v[b] = y, None
        return
    v[a], v[b] = jnp.maximum(x, y), jnp.minimum(x, y)


def _top16_sorted(tiles):
    v = list(tiles) + [None] * (PEER_TOPK - len(tiles))
    for a, b in _SORT16:
        _compare_exchange(v, a, b)
    for shift in (4, 2, 1):
        other = [None if x is None else pltpu.roll(x, shift, 0) for x in v]
        merged = []
        for i in range(PEER_TOPK):
            x, y = v[i], other[PEER_TOPK - 1 - i]
            merged.append(y if x is None else x if y is None else jnp.maximum(x, y))
        v = merged
        dist = PEER_TOPK // 2
        while dist >= 1:
            for i in range(PEER_TOPK):
                if (i & dist) == 0:
                    _compare_exchange(v, i, i + dist)
            dist //= 2
    return v


def _rows_to_tile(rows, sub):
    out = rows[0]
    for r in range(1, SUBLANES):
        out = jnp.where(sub == r, rows[r], out)
    return out


def _select_chunk(s1_tiles, s2_tiles):
    v1 = _top16_sorted(s1_tiles)
    v2 = _top16_sorted(s2_tiles)
    sub = lax.broadcasted_iota(jnp.int32, (SUBLANES, LANES), 0)
    v2_lo = _rows_to_tile(v2[0:8], sub)
    v2_hi = _rows_to_tile(v2[8:16], sub)
    v1_hi = _rows_to_tile(v1[8:16], sub)
    cands = [v1[0] + v2_lo, v1[0] + v2_hi]
    for i in range(1, 8):
        cands.append(jnp.where(sub < _PAIR_COLS[i], v1[i] + v2_lo, NEG_INF))
    cands.append(v1_hi + v2[0])
    tau = _top16_sorted(cands)[PEER_TOPK - 1]
    best = v1[0] + v2[0]
    z_tile = None
    count_rank = []
    for idx, c in enumerate(cands[:9]):
        sel = c >= tau
        e = jnp.where(sel, jnp.exp(c - best), 0.0)
        z_tile = e if z_tile is None else z_tile + e
        n = jnp.where(sel, 1.0, 0.0)
        if idx == 1:
            count_rank[0] = count_rank[0] + n
        else:
            count_rank.append(n)
    z_tile = z_tile + jnp.where(cands[9] >= tau, jnp.exp(cands[9] - best), 0.0)
    inv_z = 1.0 / jnp.broadcast_to(jnp.sum(z_tile, axis=0, keepdims=True), (SUBLANES, LANES))
    count_rank = [jnp.broadcast_to(jnp.sum(n, axis=0, keepdims=True), (SUBLANES, LANES)) for n in count_rank]
    rank2, p2, count1, p1 = [], [], [], []
    for s1, s2 in zip(s1_tiles, s2_tiles):
        n = jnp.where(s1 + v2[0] >= tau, 1.0, 0.0)
        for i in range(8):
            n = jnp.where(s1 == v1[i], count_rank[i], n)
        count1.append(n)
        p1.append(jnp.exp(s1 - v1[0]) * inv_z)
        r = jnp.full((SUBLANES, LANES), float(PEER_TOPK), F32)
        for jr in reversed(range(PEER_TOPK)):
            r = jnp.where(s2 == v2[jr], float(jr), r)
        rank2.append(r)
        p2.append(jnp.exp(s2 - v2[0]))
    return rank2, p2, count1, p1


def _peer_sel_kernel(h_ref, wq_ref, keys_ref, r2_ref, p2_ref, n1_ref, p1_ref, sc_ref):
    q = _dot(h_ref[...], wq_ref[...])
    for p in range(2):
        sc_ref[p] = _dot_nt(keys_ref[p], q[:, p * 128:(p + 1) * 128].astype(BF16))

    def chunk(j, carry):
        lanes = pl.ds(pl.multiple_of(j * LANES, LANES), LANES)
        tile = lambda p, k: sc_ref[p, k * SUBLANES:(k + 1) * SUBLANES, lanes]
        s1 = [tile(0, k) for k in range(N_KEYS // SUBLANES)]
        s2 = [tile(1, k) for k in range(N_KEYS // SUBLANES)]
        rank2, p2, count1, p1 = _select_chunk(s1, s2)
        for k in range(N_KEYS // SUBLANES):
            rows = slice(k * SUBLANES, (k + 1) * SUBLANES)
            n1_ref[j, rows, :] = count1[k]
            p1_ref[j, rows, :] = p1[k]
        for k in range(0, N_KEYS // SUBLANES, 2):
            rows = slice((k // 2) * SUBLANES, (k // 2 + 1) * SUBLANES)
            r2_ref[j, rows, :] = pltpu.bitcast(jnp.concatenate(rank2[k:k + 2], axis=0).astype(BF16), jnp.uint32)
            p2_ref[j, rows, :] = pltpu.bitcast(jnp.concatenate(p2[k:k + 2], axis=0).astype(BF16), jnp.uint32)
        return carry

    lax.fori_loop(0, SEL_TM // LANES, chunk, 0)


def _peer_select(h2, wq, keys, layer):
    tab = jax.ShapeDtypeStruct((PEER_HEADS, T_ALL // LANES, N_KEYS, LANES), F32)
    tab16 = jax.ShapeDtypeStruct((PEER_HEADS, T_ALL // LANES, N_KEYS // 2, LANES), jnp.uint32)
    tab_spec = pl.BlockSpec((None, SEL_TM // LANES, N_KEYS, LANES), lambda i, h: (h, i, 0, 0))
    tab16_spec = pl.BlockSpec((None, SEL_TM // LANES, N_KEYS // 2, LANES), lambda i, h: (h, i, 0, 0))
    return pl.pallas_call(
        _peer_sel_kernel,
        out_shape=(tab16, tab16, tab, tab),
        grid=(T_ALL // SEL_TM, PEER_HEADS),
        in_specs=[
            pl.BlockSpec((SEL_TM, D_MODEL), lambda i, h: (i, 0)),
            pl.BlockSpec((None, D_MODEL, 2 * 128), lambda i, h: (layer, 0, h)),
            pl.BlockSpec((None, None, 2, N_KEYS, 128), lambda i, h: (layer, h, 0, 0, 0)),
        ],
        out_specs=(tab16_spec, tab16_spec, tab_spec, tab_spec),
        scratch_shapes=[pltpu.VMEM((2, N_KEYS, SEL_TM), F32)],
        compiler_params=_params("arbitrary", "arbitrary"),
        name="peer_select",
    )(h2, wq, keys)


EXP_TM = 512
EXP_KEY_ROWS = 8
EXP_TE = EXP_KEY_ROWS * N_KEYS
EXP_STEPS = N_EXPERTS // EXP_TE
EXP_CHUNK_ROWS = 2
EXP_CHUNK = EXP_CHUNK_ROWS * N_KEYS
EXP_CHUNKS = EXP_TE // EXP_CHUNK


def _peer_expert_kernel(ht_ref, u_ref, unext_ref, vt_ref, r2_ref, p2_ref, n1_ref, p1_ref,
                        x_ref, mod_ref, o_ref, act_ref, gate_ref, acc_ref):
    j = pl.program_id(1)

    @pl.when(j == 0)
    def _head():
        acc_ref[...] = jnp.zeros_like(acc_ref)
        act_ref[0] = _dot(u_ref[0:EXP_CHUNK, :], ht_ref[...])

    part = None
    for c in range(EXP_CHUNKS):
        anchor = None
        for lc in range(EXP_TM // LANES):
            lanes = slice(lc * LANES, (lc + 1) * LANES)
            w = [None] * EXP_CHUNK_ROWS
            for h in range(PEER_HEADS):
                rank2 = pltpu.bitcast(r2_ref[h, lc], BF16).reshape(N_KEYS // 16, 16, LANES)
                p2 = pltpu.bitcast(p2_ref[h, lc], BF16).reshape(N_KEYS // 16, 16, LANES)
                for k in range(EXP_CHUNK_ROWS):
                    r = EXP_CHUNK_ROWS * c + k
                    count1 = jnp.broadcast_to(n1_ref[h, lc, r:r + 1, :], (16, LANES)).astype(BF16)
                    p1 = jnp.broadcast_to(p1_ref[h, lc, r:r + 1, :], (16, LANES)).astype(BF16)
                    term = jnp.where(rank2 < count1[None], p2, jnp.zeros_like(p2)) * p1[None]
                    w[k] = term if w[k] is None else w[k] + term
            for k in range(EXP_CHUNK_ROWS):
                rows = slice(k * N_KEYS, (k + 1) * N_KEYS)
                act = act_ref[c, rows, lanes].reshape(N_KEYS // 16, 16, LANES)
                g = (w[k] * _gelu_tanh(act.astype(BF16))).reshape(N_KEYS, LANES)
                gate_ref[c, rows, lanes] = g
                if anchor is None:
                    anchor = g[0:16, :]
        zero = pltpu.bitcast((pltpu.bitcast(anchor, jnp.uint32) >> 16) >> 16, BF16)
        nxt = unext_ref[...] if c == EXP_CHUNKS - 1 else u_ref[(c + 1) * EXP_CHUNK:(c + 2) * EXP_CHUNK, :]
        nxt = nxt + jnp.tile(zero, (EXP_CHUNK // 16, D_MODEL // LANES))
        act_ref[(c + 1) % EXP_CHUNKS] = _dot(nxt, ht_ref[...])
        d = _dot(vt_ref[:, c * EXP_CHUNK:(c + 1) * EXP_CHUNK], gate_ref[c])
        part = d if part is None else part + d
    acc_ref[...] += part

    @pl.when(j == EXP_STEPS - 1)
    def _tail():
        o_ref[...] = x_ref[...] + mod_ref[5:6, :] * acc_ref[...].T


def _peer_experts(h2t, u, vt, r2, p2, n1, p1, x, mod_l, layer):
    n_chunks = N_EXPERTS // EXP_CHUNK
    tab_spec = pl.BlockSpec((PEER_HEADS, EXP_TM // LANES, N_KEYS // 2, LANES), lambda i, j: (0, i, 0, 0))
    row_spec = pl.BlockSpec((PEER_HEADS, EXP_TM // LANES, EXP_KEY_ROWS, LANES), lambda i, j: (0, i, j, 0))
    return pl.pallas_call(
        _peer_expert_kernel,
        out_shape=jax.ShapeDtypeStruct((T_ALL, D_MODEL), F32),
        grid=(T_ALL // EXP_TM, EXP_STEPS),
        in_specs=[
            pl.BlockSpec((D_MODEL, EXP_TM), lambda i, j: (0, i)),
            pl.BlockSpec((None, EXP_TE, D_MODEL), lambda i, j: (layer, j, 0)),
            pl.BlockSpec((None, EXP_CHUNK, D_MODEL),
                         lambda i, j: (layer, jnp.minimum((j + 1) * EXP_CHUNKS, n_chunks - 1), 0)),
            pl.BlockSpec((None, D_MODEL, EXP_TE), lambda i, j: (layer, 0, j)),
            tab_spec, tab_spec, row_spec, row_spec,
            pl.BlockSpec((EXP_TM, D_MODEL), lambda i, j: (i, 0)),
            pl.BlockSpec((None, 6, D_MODEL), lambda i, j: (_segment(i, EXP_TM), 0, 0)),
        ],
        out_specs=pl.BlockSpec((EXP_TM, D_MODEL), lambda i, j: (i, 0)),
        scratch_shapes=[
            pltpu.VMEM((EXP_CHUNKS, EXP_CHUNK, EXP_TM), F32),
            pltpu.VMEM((EXP_CHUNKS, EXP_CHUNK, EXP_TM), BF16),
            pltpu.VMEM((D_MODEL, EXP_TM), F32),
        ],
        compiler_params=_params("arbitrary", "arbitrary"),
        name="peer_experts",
    )(h2t, u, u, vt, r2, p2, n1, p1, x, mod_l)


FIN_TM = 1024


def _final_kernel(x_ref, w_ref, o_ref):
    o_ref[...] = _rms(x_ref[...]) * w_ref[...]


def _final_norm(x, w, tok0, n_tok):
    blk0 = tok0 // FIN_TM
    return pl.pallas_call(
        _final_kernel,
        out_shape=jax.ShapeDtypeStruct((n_tok, D_MODEL), F32),
        grid=(n_tok // FIN_TM,),
        in_specs=[pl.BlockSpec((FIN_TM, D_MODEL), lambda i: (blk0 + i, 0)),
                  pl.BlockSpec((1, D_MODEL), lambda i: (0, 0))],
        out_specs=pl.BlockSpec((FIN_TM, D_MODEL), lambda i: (i, 0)),
        compiler_params=_params("arbitrary"),
        name="final_norm",
    )(x, w)


def kernel(x_prompt, x_sample, state_C, state_n, state_m, c, c_ctx, w_mod, b_mod, w_in, b_gate,
           conv_w, w_s, b_s, mlstm_norm_w, w_out, peer_wq, peer_keys, peer_u, peer_v, final_norm_w):
    assert x_prompt.shape == (N_CTX_SEQ, CTX_LEN, D_MODEL)
    assert x_sample.shape == (N_LAT_SEQ, LAT_LEN, D_MODEL)
    x = jnp.concatenate([x_prompt.reshape(T_CTX, D_MODEL), x_sample.reshape(T_LAT, D_MODEL)], axis=0)

    cond = jnp.zeros((8, D_MODEL), F32).at[0].set(c_ctx).at[1:1 + N_LAT_SEQ].set(c)
    mod = _modulation(cond, w_mod, b_mod)[:, :1 + N_LAT_SEQ].reshape(DEPTH, 1 + N_LAT_SEQ, 6, D_MODEL)

    zero_c = jnp.zeros((N_CTX_SEQ, 2, N_HEADS, DH, DH), F32)
    zero_n = jnp.zeros((N_CTX_SEQ, 2, N_HEADS, DH, 1), F32)
    zero_m = jnp.zeros((N_CTX_SEQ, 2 * N_HEADS, DH), F32)

    w_in_b = w_in.astype(BF16)
    w_s_b = w_s.astype(BF16)
    w_out_b = w_out.astype(BF16)
    wq_b = peer_wq.astype(BF16)
    keys_b = peer_keys.astype(BF16)
    u_b = peer_u.astype(BF16)
    vt_b = jnp.swapaxes(peer_v, 1, 2).astype(BF16)

    new_c, new_n, new_m = [], [], []
    for l in range(DEPTH):
        w_gate = w_in[l, :, D_PROJ:]
        wg = jnp.pad(w_gate, ((0, 0), (0, GATE_PAD - N_GATES))).astype(BF16)
        wgt = w_gate.T.astype(BF16)
        bg = jnp.pad(b_gate[l].reshape(1, N_GATES), ((0, 0), (0, GATE_PAD - N_GATES)))
        bgt = b_gate[l].reshape(N_GATES, 1)
        y, g, gt = _inproj(x, mod[l], w_in_b, wg, wgt, bg, bgt, l)

        qk = _conv(y, conv_w[l].reshape(9, 2 * D_MLSTM))

        hf_c, hb_c, c_ctx_new, n_ctx_new, m_ctx_new = _mlstm(
            qk, y, g, gt, zero_c, zero_n, zero_m, N_CTX_SEQ, CTX_LEN, 0)
        lat_m0 = jnp.broadcast_to(state_m[:, l].reshape(N_LAT_SEQ, 2 * N_HEADS, 1),
                                  (N_LAT_SEQ, 2 * N_HEADS, DH))
        hf_l, hb_l, _, _, _ = _mlstm(
            qk, y, g, gt, state_C[:, l], state_n[:, l][..., None], lat_m0, N_LAT_SEQ, LAT_LEN, T_CTX)
        new_c.append(c_ctx_new)
        new_n.append(n_ctx_new[..., 0])
        new_m.append(m_ctx_new[..., 0].reshape(N_CTX_SEQ, 2, N_HEADS))

        bs_b = jnp.broadcast_to(b_s[l][:, :, None], (N_HEADS, CHUNK, DH))
        x, h2, h2t = _mix(x, y, hf_c, hb_c, hf_l, hb_l, mod[l], w_s_b, bs_b,
                          mlstm_norm_w[l].reshape(1, D_MLSTM), w_out_b, l)

        rank2, p2, count1, p1 = _peer_select(h2, wq_b, keys_b, l)
        x = _peer_experts(h2t, u_b, vt_b, rank2, p2, count1, p1, x, mod[l], l)

    fin_w = final_norm_w.reshape(1, D_MODEL)
    y_prompt = _final_norm(x, fin_w, 0, T_CTX).reshape(N_CTX_SEQ, CTX_LEN, D_MODEL)
    y_sample = _final_norm(x, fin_w, T_CTX, T_LAT).reshape(N_LAT_SEQ, LAT_LEN, D_MODEL)
    return (y_prompt, y_sample, jnp.stack(new_c, axis=1), jnp.stack(new_n, axis=1),
            jnp.stack(new_m, axis=1))
```
